```python
import math
import jax, jax.numpy as jnp
from jax import lax
import numpy as np

D_MODEL = 2048
BATCH = 2
SEQ = 16384
DEPTH = 2

ATTN_HEAD_DIM = 128
ATTN_WIDTH = D_MODEL // 2
ATTN_HEADS = ATTN_WIDTH // ATTN_HEAD_DIM
SC_WIDTH = D_MODEL // 4
SC_KERNEL = 3
CF_WIDTH = D_MODEL // 4
CF_KERNEL = 31
Q_BLOCK = 128
FORGET_BIAS_INIT = 4.0
IN_SPLITS = (ATTN_WIDTH, ATTN_WIDTH, ATTN_WIDTH, ATTN_HEADS, SC_WIDTH, SC_WIDTH, SC_WIDTH, CF_WIDTH, CF_WIDTH)
IN_COLS = sum(IN_SPLITS)
N_EXPERTS = 64
TOP_K = 8
N_GROUPS = 8
TOPK_GROUPS = 4
EXPERT_HIDDEN = D_MODEL // 4
SHARED_HIDDEN = D_MODEL // 4
ROUTED_SCALE = 2.5
EXPERT_BLOCK = 256
N_MOD = 6
NORM_EPS = 1e-6

kernel_name = 'hybrid_fox_conv_conformer_moe_adaln'


def rms_norm(x, g):
    xf = x.astype(jnp.float32)
    y = xf * lax.rsqrt(jnp.mean(xf * xf, axis=-1, keepdims=True) + NORM_EPS)
    return (y * g.astype(jnp.float32)).astype(x.dtype)


def layer_norm(x, g, b):
    xf = x.astype(jnp.float32)
    mu = jnp.mean(xf, axis=-1, keepdims=True)
    var = jnp.mean(jnp.square(xf - mu), axis=-1, keepdims=True)
    y = (xf - mu) * lax.rsqrt(var + NORM_EPS)
    return (y * g.astype(jnp.float32) + b.astype(jnp.float32)).astype(x.dtype)


def causal_depthwise_conv(u, w):
    k_width = w.shape[0]
    return lax.conv_general_dilated(
        u, w[:, None, :].astype(u.dtype), window_strides=(1,), padding=[(k_width - 1, 0)],
        dimension_numbers=('NWC', 'WIO', 'NWC'), feature_group_count=u.shape[-1])


def forgetting_attention(q, k, v, log_f):
    b, s, h, dh = q.shape
    nb = s // Q_BLOCK
    cum = jnp.cumsum(log_f, axis=1).transpose(0, 2, 1)
    q_blocks = (q * (dh ** -0.5)).reshape(b, nb, Q_BLOCK, h, dh).swapaxes(0, 1)
    cum_blocks = cum.reshape(b, h, nb, Q_BLOCK).transpose(2, 0, 1, 3)
    key_pos = jnp.arange(s)

    def one_block(args):
        qb, cb, i = args
        logits = jnp.einsum('bqhd,bkhd->bhqk', qb, k, preferred_element_type=jnp.float32)
        logits = logits + cb[..., :, None] - cum[..., None, :]
        q_pos = i * Q_BLOCK + jnp.arange(Q_BLOCK)
        logits = jnp.where(key_pos[None, :] <= q_pos[:, None], logits, -jnp.inf)
        p = jax.nn.softmax(logits, axis=-1)
        return jnp.einsum('bhqk,bkhd->bqhd', p.astype(v.dtype), v)

    out = lax.map(one_block, (q_blocks, cum_blocks, jnp.arange(nb)))
    return out.swapaxes(0, 1).reshape(b, s, h, dh)


def hybrid_mixer(h, w_in, b_forget, sc_conv_w, cf_conv_w, cf_conv_b, cf_ln_g, cf_ln_b, out_norm_g, w_out):
    b, s, _ = h.shape
    proj = h @ w_in
    q, k, v, f_logit, sc_b, sc_c, sc_u, cf_a, cf_gate = jnp.split(
        proj, np.cumsum(IN_SPLITS)[:-1].tolist(), axis=-1)
    heads = lambda t: t.reshape(b, s, ATTN_HEADS, ATTN_HEAD_DIM)
    log_f = jax.nn.log_sigmoid((f_logit + b_forget).astype(jnp.float32))
    y_attn = forgetting_attention(heads(q), heads(k), heads(v), log_f).reshape(b, s, ATTN_WIDTH)
    y_sc = sc_b * causal_depthwise_conv(sc_c * sc_u, sc_conv_w)
    u = cf_a * jax.nn.sigmoid(cf_gate)
    u = causal_depthwise_conv(u, cf_conv_w) + cf_conv_b
    y_cf = jax.nn.silu(layer_norm(u, cf_ln_g, cf_ln_b))
    g_attn, g_sc, g_cf = jnp.split(out_norm_g, [ATTN_WIDTH, ATTN_WIDTH + SC_WIDTH])
    y = jnp.concatenate([rms_norm(y_attn, g_attn), rms_norm(y_sc, g_sc), rms_norm(y_cf, g_cf)], axis=-1)
    return y @ w_out


def moe_ffn(h, w_router, router_bias, w_gate, w_up, w_down, ws_gate, ws_up, ws_down):
    b, s, d = h.shape
    t = b * s
    xt = h.reshape(t, d)
    scores = jax.nn.sigmoid((xt @ w_router).astype(jnp.float32))
    biased = scores + router_bias.astype(jnp.float32)
    grp = biased.reshape(t, N_GROUPS, N_EXPERTS // N_GROUPS)
    grp_score = lax.top_k(grp, 2)[0].sum(-1)
    gidx = lax.top_k(grp_score, TOPK_GROUPS)[1]
    gmask = jnp.any(gidx[..., None] == jnp.arange(N_GROUPS), axis=-2)
    emask = jnp.repeat(gmask, N_EXPERTS // N_GROUPS, axis=-1)
    eidx = lax.top_k(jnp.where(emask, biased, -jnp.inf), TOP_K)[1]
    gate = jnp.take_along_axis(scores, eidx, axis=-1)
    gate = gate / (gate.sum(-1, keepdims=True) + 1e-20) * ROUTED_SCALE
    n_assign = t * TOP_K
    n_blocks = -(-(n_assign + N_EXPERTS * (EXPERT_BLOCK - 1)) // EXPERT_BLOCK)
    n_rows = n_blocks * EXPERT_BLOCK
    flat_e = eidx.reshape(-1)
    order = jnp.argsort(flat_e)
    sorted_e = flat_e[order]
    counts = jnp.bincount(flat_e, length=N_EXPERTS)
    padded = (counts + EXPERT_BLOCK - 1) // EXPERT_BLOCK * EXPERT_BLOCK
    pad_end = jnp.cumsum(padded)
    start = jnp.cumsum(counts) - counts
    dest = (pad_end - padded)[sorted_e] + jnp.arange(n_assign) - start[sorted_e]
    row_tok = jnp.full((n_rows,), t, jnp.int32).at[dest].set((order // TOP_K).astype(jnp.int32))
    row_w = jnp.zeros((n_rows,), jnp.float32).at[dest].set(gate.reshape(-1)[order])
    block_expert = jnp.minimum(
        jnp.searchsorted(pad_end, jnp.arange(n_blocks) * EXPERT_BLOCK, side='right'), N_EXPERTS - 1)
    xpad = jnp.concatenate([xt, jnp.zeros((1, d), xt.dtype)], axis=0)

    def expert_block(out, blk):
        tok, wrow, e = blk
        xb = xpad[tok]
        hb = jax.nn.silu(xb @ w_gate[e]) * (xb @ w_up[e])
        yb = (hb @ w_down[e]) * wrow[:, None].astype(hb.dtype)
        return out.at[tok].add(yb.astype(out.dtype)), None

    routed, _ = lax.scan(expert_block, jnp.zeros((t + 1, d), xt.dtype),
                         (row_tok.reshape(n_blocks, EXPERT_BLOCK), row_w.reshape(n_blocks, EXPERT_BLOCK), block_expert))
    shared = (jax.nn.silu(xt @ ws_gate) * (xt @ ws_up)) @ ws_down
    return (routed[:t] + shared).reshape(b, s, d)


def setup_inputs(seed: int = 0) -> dict:
    key = jax.random.key(seed)
    ks = jax.random.split(key, 26)
    nrm = lambda k, shape, scale: jax.random.normal(k, shape, jnp.float32) * scale
    L, D = DEPTH, D_MODEL
    return {
        'x': nrm(ks[0], (BATCH, SEQ, D), 1.0),
        'c': nrm(ks[1], (BATCH, D), 1.0),
        'w_ada': nrm(ks[2], (L, D, N_MOD * D), 0.5 * D ** -0.5),
        'b_ada': nrm(ks[3], (L, N_MOD * D), 0.02),
        'norm1_g': 1.0 + nrm(ks[4], (L, D), 0.02),
        'norm2_g': 1.0 + nrm(ks[5], (L, D), 0.02),
        'w_in': nrm(ks[6], (L, D, IN_COLS), D ** -0.5),
        'b_forget': FORGET_BIAS_INIT + nrm(ks[7], (L, ATTN_HEADS), 0.5),
        'sc_conv_w': nrm(ks[8], (L, SC_KERNEL, SC_WIDTH), SC_KERNEL ** -0.5),
        'cf_conv_w': nrm(ks[9], (L, CF_KERNEL, CF_WIDTH), CF_KERNEL ** -0.5),
        'cf_conv_b': nrm(ks[10], (L, CF_WIDTH), 0.02),
        'cf_ln_g': 1.0 + nrm(ks[11], (L, CF_WIDTH), 0.02),
        'cf_ln_b': nrm(ks[12], (L, CF_WIDTH), 0.02),
        'out_norm_g': 1.0 + nrm(ks[13], (L, D), 0.02),
        'w_out': nrm(ks[14], (L, D, D), D ** -0.5),
        'w_router': nrm(ks[15], (L, D, N_EXPERTS), D ** -0.5),
        'router_bias': nrm(ks[16], (L, N_EXPERTS), 0.01),
        'w_gate': nrm(ks[17], (L, N_EXPERTS, D, EXPERT_HIDDEN), D ** -0.5),
        'w_up': nrm(ks[18], (L, N_EXPERTS, D, EXPERT_HIDDEN), D ** -0.5),
        'w_down': nrm(ks[19], (L, N_EXPERTS, EXPERT_HIDDEN, D), EXPERT_HIDDEN ** -0.5),
        'ws_gate': nrm(ks[20], (L, D, SHARED_HIDDEN), D ** -0.5),
        'ws_up': nrm(ks[21], (L, D, SHARED_HIDDEN), D ** -0.5),
        'ws_down': nrm(ks[22], (L, SHARED_HIDDEN, D), SHARED_HIDDEN ** -0.5),
        'final_norm_g': 1.0 + nrm(ks[23], (D,), 0.02),
    }


def reference(x, c, w_ada, b_ada, norm1_g, norm2_g, w_in, b_forget, sc_conv_w, cf_conv_w, cf_conv_b,
              cf_ln_g, cf_ln_b, out_norm_g, w_out, w_router, router_bias, w_gate, w_up, w_down,
              ws_gate, ws_up, ws_down, final_norm_g):
    c_act = jax.nn.silu(c)
    for l in range(DEPTH):
        mod = c_act @ w_ada[l] + b_ada[l]
        sh1, sc1, g1, sh2, sc2, g2 = [m[:, None, :] for m in jnp.split(mod, N_MOD, axis=-1)]
        h = rms_norm(x, norm1_g[l]) * (1.0 + sc1) + sh1
        x = x + g1 * hybrid_mixer(h, w_in[l], b_forget[l], sc_conv_w[l], cf_conv_w[l], cf_conv_b[l],
                                  cf_ln_g[l], cf_ln_b[l], out_norm_g[l], w_out[l])
        h = rms_norm(x, norm2_g[l]) * (1.0 + sc2) + sh2
        x = x + g2 * moe_ffn(h, w_router[l], router_bias[l], w_gate[l], w_up[l], w_down[l],
                             ws_gate[l], ws_up[l], ws_down[l])
    return rms_norm(x, final_norm_g)
```

```python
import functools
import math

import jax
import jax.numpy as jnp
from jax import lax
from jax.experimental import pallas as pl
from jax.experimental.pallas import tpu as pltpu

F32 = jnp.float32
BF16 = jnp.bfloat16
I32 = jnp.int32
U32 = jnp.uint32

V7X_LANES = 128
V7X_SUBLANES = 8
V7X_VMEM_BYTES = 64 * 1024 * 1024
VMEM_LIMIT = V7X_VMEM_BYTES - 8 * 1024 * 1024

HEAD_DIM = 128
N_EXPERTS = 64
TOP_K = 8
N_GROUPS = 8
GROUP_SIZE = N_EXPERTS // N_GROUPS
TOPK_GROUPS = 4
ROUTED_SCALE = 2.5
EXPERT_BLOCK = 256
SC_KERNEL = 3
CF_KERNEL = 31
N_MOD = 6
NORM_EPS = 1e-6
SC_HALO = 8
CF_HALO = 32
NEG_INF = float("-inf")


def _params(semantics, **kw):
    return pltpu.CompilerParams(dimension_semantics=semantics, vmem_limit_bytes=VMEM_LIMIT, **kw)


def _tiles(seq):
    return dict(
        tm_in=min(1024, seq),
        tn_in=512,
        tc=min(512, seq),
        tq=min(512, seq),
        tm_mix=min(256, seq),
        tr=min(512, seq),
        td=min(256, seq),
    )


def _rms(x, g):
    return x * lax.rsqrt(jnp.mean(x * x, axis=-1, keepdims=True) + NORM_EPS) * g


def _silu(x):
    return x * jax.nn.sigmoid(x)


def _pack_bf16_pair(lo, hi):
    lo_b = lax.bitcast_convert_type(lo.astype(BF16).astype(F32), U32) >> 16
    hi_b = lax.bitcast_convert_type(hi.astype(BF16).astype(F32), U32) & jnp.uint32(0xFFFF0000)
    return hi_b | lo_b


def _unpack_bf16_pair(p):
    lo = lax.bitcast_convert_type(p << 16, F32)
    hi = lax.bitcast_convert_type(p & jnp.uint32(0xFFFF0000), F32)
    return lo, hi


def _split3(x):
    a = x.astype(BF16)
    r = x - a.astype(F32)
    b = r.astype(BF16)
    c = (r - b.astype(F32)).astype(BF16)
    return a, b, c


def _dot_nt(a, b):
    return lax.dot_general(a, b, (((1,), (1,)), ((), ())), preferred_element_type=F32)


def _ada_kernel(c_ref, w_ref, b_ref, o_ref):
    c = c_ref[...]
    ca = _silu(c).astype(BF16)
    o_ref[0] = jnp.dot(ca, w_ref[0].astype(BF16), preferred_element_type=F32) + b_ref[0]


def _ada(c_pad, w_ada, b_ada):
    depth, d, n = w_ada.shape
    tn = 1024
    return pl.pallas_call(
        _ada_kernel,
        grid=(depth, n // tn),
        in_specs=[
            pl.BlockSpec((V7X_SUBLANES, d), lambda l, j: (0, 0)),
            pl.BlockSpec((1, d, tn), lambda l, j: (l, 0, j)),
            pl.BlockSpec((1, 1, tn), lambda l, j: (l, 0, j)),
        ],
        out_specs=pl.BlockSpec((1, V7X_SUBLANES, tn), lambda l, j: (l, 0, j)),
        out_shape=jax.ShapeDtypeStruct((depth, V7X_SUBLANES, n), F32),
        compiler_params=_params(("arbitrary", "arbitrary")),
        name="ada_mod",
    )(c_pad, w_ada, b_ada.reshape(depth, 1, n))


def _inproj_kernel(x_ref, g_ref, sh_ref, sc_ref, w_ref, wf_ref, wft_ref,
                   qkv_ref, conv_ref, fr_ref, ft_ref, h_scr, *, n_q_steps, n_qkv_steps, heads_per_step):
    j = pl.program_id(1)

    @pl.when(j == 0)
    def _():
        h = _rms(x_ref[...], g_ref[...]) * (1.0 + sc_ref[0]) + sh_ref[0]
        hb = h.astype(BF16)
        h_scr[...] = hb
        fr_ref[...] = jnp.dot(hb, wf_ref[...], preferred_element_type=F32)
        ft_ref[...] = _dot_nt(wft_ref[...], hb)

    acc = jnp.dot(h_scr[...], w_ref[...], preferred_element_type=F32)

    @pl.when(j < n_qkv_steps)
    def _():
        scale = jnp.where(j < n_q_steps, HEAD_DIM ** -0.5, 1.0).astype(F32)
        a = acc * scale
        for u in range(heads_per_step):
            qkv_ref[u] = a[:, u * HEAD_DIM:(u + 1) * HEAD_DIM].astype(BF16)

    @pl.when(j >= n_qkv_steps)
    def _():
        conv_ref[...] = acc.astype(BF16)


def _inproj(x2, g1, mod, w_main, w_f, w_ft, *, seq, attn_width, conv_width, tm, tn):
    t, d = x2.shape
    nb = seq // tm
    n_q_steps = attn_width // tn
    n_qkv_steps = 3 * attn_width // tn
    n_steps = n_qkv_steps + conv_width // tn
    hps = tn // HEAD_DIM
    n_heads3 = 3 * attn_width // HEAD_DIM
    kern = functools.partial(_inproj_kernel, n_q_steps=n_q_steps, n_qkv_steps=n_qkv_steps,
                             heads_per_step=hps)
    return pl.pallas_call(
        kern,
        grid=(t // tm, n_steps),
        in_specs=[
            pl.BlockSpec((tm, d), lambda i, j: (i, 0)),
            pl.BlockSpec((1, d), lambda i, j: (0, 0)),
            pl.BlockSpec((1, 1, d), lambda i, j: (i // nb, 0, 0)),
            pl.BlockSpec((1, 1, d), lambda i, j: (i // nb, 0, 1)),
            pl.BlockSpec((d, tn), lambda i, j: (0, j)),
            pl.BlockSpec((d, V7X_LANES), lambda i, j: (0, 0)),
            pl.BlockSpec((2 * V7X_SUBLANES, d), lambda i, j: (0, 0)),
        ],
        out_specs=[
            pl.BlockSpec((hps, tm, HEAD_DIM), lambda i, j: (jnp.minimum(j, n_qkv_steps - 1), i, 0)),
            pl.BlockSpec((tm, tn), lambda i, j: (i, jnp.maximum(j - n_qkv_steps, 0))),
            pl.BlockSpec((tm, V7X_LANES), lambda i, j: (i, 0)),
            pl.BlockSpec((2 * V7X_SUBLANES, tm), lambda i, j: (0, i)),
        ],
        out_shape=[
            jax.ShapeDtypeStruct((n_heads3, t, HEAD_DIM), BF16),
            jax.ShapeDtypeStruct((t, conv_width), BF16),
            jax.ShapeDtypeStruct((t, V7X_LANES), F32),
            jax.ShapeDtypeStruct((2 * V7X_SUBLANES, t), F32),
        ],
        scratch_shapes=[pltpu.VMEM((tm, d), BF16)],
        compiler_params=_params(("arbitrary", "arbitrary")),
        name="norm_inproj",
    )(x2, g1, mod, mod, w_main, w_f, w_ft)


def _log_sigmoid(z):
    return jnp.minimum(z, 0.0) - jnp.log1p(jnp.exp(-jnp.abs(z)))


def _fcum_kernel(fr_ref, ft_ref, br_ref, bt_ref, cr_ref, ct_ref, car_r, car_t, *, nb, tc):
    i = pl.program_id(0)

    @pl.when(i % nb == 0)
    def _():
        car_r[...] = jnp.zeros_like(car_r)
        car_t[...] = jnp.zeros_like(car_t)

    row = lax.broadcasted_iota(I32, (tc, tc), 0)
    col = lax.broadcasted_iota(I32, (tc, tc), 1)
    lower = (row >= col).astype(BF16)
    upper = (row <= col).astype(BF16)

    lf_r = _log_sigmoid(fr_ref[...] + br_ref[...])
    cum_r = car_r[0:1, :]
    for part in _split3(lf_r):
        cum_r = cum_r + jnp.dot(lower, part, preferred_element_type=F32)
    cr_ref[...] = cum_r
    car_r[...] = jnp.broadcast_to(cum_r[tc - 1:tc, :], car_r.shape)

    lf_t = _log_sigmoid(ft_ref[...] + bt_ref[:, 0:1])
    cum_t = car_t[:, 0:1]
    for part in _split3(lf_t):
        cum_t = cum_t + jnp.dot(part, upper, preferred_element_type=F32)
    ct_ref[...] = cum_t
    car_t[...] = jnp.broadcast_to(cum_t[:, tc - 1:tc], car_t.shape)


def _fcum(f_rows, f_t, bias_row, bias_col, *, seq, tc):
    t = f_rows.shape[0]
    nb = seq // tc
    rows_t = f_t.shape[0]
    return pl.pallas_call(
        functools.partial(_fcum_kernel, nb=nb, tc=tc),
        grid=(t // tc,),
        in_specs=[
            pl.BlockSpec((tc, V7X_LANES), lambda i: (i, 0)),
            pl.BlockSpec((rows_t, tc), lambda i: (0, i)),
            pl.BlockSpec((1, V7X_LANES), lambda i: (0, 0)),
            pl.BlockSpec((rows_t, V7X_LANES), lambda i: (0, 0)),
        ],
        out_specs=[
            pl.BlockSpec((tc, V7X_LANES), lambda i: (i, 0)),
            pl.BlockSpec((rows_t, tc), lambda i: (0, i)),
        ],
        out_shape=[
            jax.ShapeDtypeStruct((t, V7X_LANES), F32),
            jax.ShapeDtypeStruct((rows_t, t), F32),
        ],
        scratch_shapes=[pltpu.VMEM((V7X_SUBLANES, V7X_LANES), F32),
                        pltpu.VMEM((rows_t, V7X_LANES), F32)],
        compiler_params=_params(("arbitrary",)),
        name="forget_cumsum",
    )(f_rows, f_t, bias_row, bias_col)


def _attn_kernel(q_ref, k_ref, v_ref, fq_ref, fk_ref, o_ref, m_scr, l_scr, acc_scr, *, tq):
    h = pl.program_id(1)
    i = pl.program_id(2)
    q = q_ref[0]
    lane = lax.broadcasted_iota(I32, (tq, V7X_LANES), 1)
    fq = jnp.sum(jnp.where(lane == h, fq_ref[...], 0.0), axis=-1, keepdims=True)

    m_scr[...] = jnp.full_like(m_scr, NEG_INF)
    l_scr[...] = jnp.zeros_like(l_scr)
    acc_scr[...] = jnp.zeros_like(acc_scr)

    def step(j, masked):
        start = pl.multiple_of(j * tq, tq)
        k = k_ref[0, pl.ds(start, tq), :]
        v = v_ref[0, pl.ds(start, tq), :]
        fk = fk_ref[pl.ds(h, 1), pl.ds(start, tq)]
        s = _dot_nt(q, k) + fq - fk
        if masked:
            r = lax.broadcasted_iota(I32, (tq, tq), 0)
            c = lax.broadcasted_iota(I32, (tq, tq), 1)
            s = jnp.where(c <= r, s, NEG_INF)
        m_prev = m_scr[...]
        m_new = jnp.maximum(m_prev, jnp.max(s, axis=-1, keepdims=True))
        alpha = jnp.exp(m_prev - m_new)
        p = jnp.exp(s - m_new)
        l_scr[...] = alpha * l_scr[...] + jnp.sum(p, axis=-1, keepdims=True)
        acc_scr[...] = alpha * acc_scr[...] + jnp.dot(p.astype(BF16), v, preferred_element_type=F32)
        m_scr[...] = m_new

    def body(j, carry):
        step(j, False)
        return carry

    lax.fori_loop(0, i, body, 0)
    step(i, True)
    o_ref[...] = acc_scr[...] / l_scr[...]


def _attention(qkv, cum_rows, cum_t, *, batch, seq, n_heads, tq):
    t = batch * seq
    nq = seq // tq
    rows_t = cum_t.shape[0]
    return pl.pallas_call(
        functools.partial(_attn_kernel, tq=tq),
        grid=(batch, n_heads, nq),
        in_specs=[
            pl.BlockSpec((1, tq, HEAD_DIM), lambda b, h, i: (h, b * nq + i, 0)),
            pl.BlockSpec((1, seq, HEAD_DIM), lambda b, h, i: (n_heads + h, b, 0)),
            pl.BlockSpec((1, seq, HEAD_DIM), lambda b, h, i: (2 * n_heads + h, b, 0)),
            pl.BlockSpec((tq, V7X_LANES), lambda b, h, i: (b * nq + i, 0)),
            pl.BlockSpec((rows_t, seq), lambda b, h, i: (0, b)),
        ],
        out_specs=pl.BlockSpec((tq, HEAD_DIM), lambda b, h, i: (b * nq + i, h)),
        out_shape=jax.ShapeDtypeStruct((t, n_heads * HEAD_DIM), F32),
        scratch_shapes=[pltpu.VMEM((tq, 1), F32), pltpu.VMEM((tq, 1), F32),
                        pltpu.VMEM((tq, HEAD_DIM), F32)],
        compiler_params=_params(("arbitrary", "arbitrary", "arbitrary")),
        name="forget_attention",
    )(qkv, qkv, qkv, cum_rows, cum_t)


def _mixer_out_kernel(x_ref, ya_ref, cv_ref, scw_ref, cfw_ref, cfb_ref, lng_ref, lnb_ref, ong_ref,
                      wout_ref, g1_ref, n2g_ref, sh2_ref, sc2_ref, wr_hi_ref, wr_lo_ref,
                      xo_ref, hp_ref, lg_ref, zbuf, ubuf, *, nb, tm, sc_w, cf_w, attn_w):
    i = pl.program_id(0)

    @pl.when(i % nb == 0)
    def _():
        zbuf[0:SC_HALO, :] = jnp.zeros((SC_HALO, sc_w), F32)
        ubuf[0:CF_HALO, :] = jnp.zeros((CF_HALO, cf_w), F32)

    cv = cv_ref[...].astype(F32)
    sc_b = cv[:, 0:sc_w]
    sc_c = cv[:, sc_w:2 * sc_w]
    sc_u = cv[:, 2 * sc_w:3 * sc_w]
    cf_a = cv[:, 3 * sc_w:3 * sc_w + cf_w]
    cf_gate = cv[:, 3 * sc_w + cf_w:3 * sc_w + 2 * cf_w]

    zbuf[SC_HALO:SC_HALO + tm, :] = sc_c * sc_u
    conv = jnp.zeros((tm, sc_w), F32)
    for k in range(SC_KERNEL):
        off = SC_HALO - (SC_KERNEL - 1) + k
        conv = conv + scw_ref[k:k + 1, :] * zbuf[off:off + tm, :]
    y_sc = sc_b * conv
    zbuf[0:SC_HALO, :] = zbuf[tm:tm + SC_HALO, :]

    ubuf[CF_HALO:CF_HALO + tm, :] = cf_a * jax.nn.sigmoid(cf_gate)
    conv = jnp.zeros((tm, cf_w), F32) + cfb_ref[...]
    for k in range(CF_KERNEL):
        off = CF_HALO - (CF_KERNEL - 1) + k
        conv = conv + cfw_ref[k:k + 1, :] * ubuf[off:off + tm, :]
    ubuf[0:CF_HALO, :] = ubuf[tm:tm + CF_HALO, :]
    mu = jnp.mean(conv, axis=-1, keepdims=True)
    cen = conv - mu
    var = jnp.mean(cen * cen, axis=-1, keepdims=True)
    y_cf = _silu(cen * lax.rsqrt(var + NORM_EPS) * lng_ref[...] + lnb_ref[...])

    ong = ong_ref[...]
    y = jnp.concatenate([
        _rms(ya_ref[...], ong[:, 0:attn_w]),
        _rms(y_sc, ong[:, attn_w:attn_w + sc_w]),
        _rms(y_cf, ong[:, attn_w + sc_w:attn_w + sc_w + cf_w]),
    ], axis=-1).astype(BF16)
    mix = jnp.dot(y, wout_ref[...], preferred_element_type=F32)
    x_new = x_ref[...] + g1_ref[0] * mix
    xo_ref[...] = x_new

    h2 = _rms(x_new, n2g_ref[...]) * (1.0 + sc2_ref[0]) + sh2_ref[0]
    half = h2.shape[-1] // 2
    hp_ref[...] = _pack_bf16_pair(h2[:, :half], h2[:, half:])
    h_hi = h2.astype(BF16)
    h_lo = (h2 - h_hi.astype(F32)).astype(BF16)
    w_hi = wr_hi_ref[...]
    lg_ref[...] = _dot_nt(w_hi, h_hi) + _dot_nt(w_hi, h_lo) + _dot_nt(wr_lo_ref[...], h_hi)


def _mixer_out(x2, y_attn, conv_in, scw, cfw, cfb, lng, lnb, ong, w_out, mod, n2g, wr_hi, wr_lo,
               *, seq, tm):
    t, d = x2.shape
    attn_w = y_attn.shape[1]
    sc_w = scw.shape[1]
    cf_w = cfw.shape[1]
    conv_w = conv_in.shape[1]
    nb = seq // tm
    n_exp = wr_hi.shape[0]
    full = lambda a: pl.BlockSpec(a.shape, lambda i: (0,) * a.ndim)
    modspec = lambda c: pl.BlockSpec((1, 1, d), lambda i: (i // nb, 0, c))
    kern = functools.partial(_mixer_out_kernel, nb=nb, tm=tm, sc_w=sc_w, cf_w=cf_w, attn_w=attn_w)
    return pl.pallas_call(
        kern,
        grid=(t // tm,),
        in_specs=[
            pl.BlockSpec((tm, d), lambda i: (i, 0)),
            pl.BlockSpec((tm, attn_w), lambda i: (i, 0)),
            pl.BlockSpec((tm, conv_w), lambda i: (i, 0)),
            full(scw), full(cfw), full(cfb), full(lng), full(lnb), full(ong), full(w_out),
            modspec(2), full(n2g), modspec(3), modspec(4), full(wr_hi), full(wr_lo),
        ],
        out_specs=[
            pl.BlockSpec((tm, d), lambda i: (i, 0)),
            pl.BlockSpec((tm, d // 2), lambda i: (i, 0)),
            pl.BlockSpec((n_exp, tm), lambda i: (0, i)),
        ],
        out_shape=[
            jax.ShapeDtypeStruct((t, d), F32),
            jax.ShapeDtypeStruct((t, d // 2), U32),
            jax.ShapeDtypeStruct((n_exp, t), F32),
        ],
        scratch_shapes=[pltpu.VMEM((SC_HALO + tm, sc_w), F32), pltpu.VMEM((CF_HALO + tm, cf_w), F32)],
        compiler_params=_params(("arbitrary",)),
        name="conv_norm_outproj",
    )(x2, y_attn, conv_in, scw, cfw, cfb, lng, lnb, ong, w_out, mod, n2g, mod, mod, wr_hi, wr_lo)


def _col_max(x):
    return jnp.max(x, axis=0, keepdims=True)


def _col_min(x):
    return jnp.min(x, axis=0, keepdims=True)


def _router_kernel(lg_ref, bias_ref, eidx_ref, gate_ref, rank_ref, cnt_ref, carry, *, tr):
    @pl.when(pl.program_id(0) == 0)
    def _():
        carry[...] = jnp.zeros_like(carry)

    scores = jax.nn.sigmoid(lg_ref[...])
    biased = scores + bias_ref[...]
    rid = lax.broadcasted_iota(I32, (GROUP_SIZE, tr), 0)
    sc_g = [scores[GROUP_SIZE * g:GROUP_SIZE * (g + 1)] for g in range(N_GROUPS)]
    bi_g = [biased[GROUP_SIZE * g:GROUP_SIZE * (g + 1)] for g in range(N_GROUPS)]

    gs = []
    for g in range(N_GROUPS):
        m1 = _col_max(bi_g[g])
        first = _col_min(jnp.where(bi_g[g] == m1, rid, GROUP_SIZE))
        m2 = _col_max(jnp.where(rid == first, NEG_INF, bi_g[g]))
        gs.append(m1 + m2)
    grp = jnp.concatenate(gs, axis=0)

    keep = jnp.zeros((N_GROUPS, tr), F32)
    for _ in range(TOPK_GROUPS):
        m = _col_max(grp)
        first = _col_min(jnp.where(grp == m, rid, N_GROUPS))
        hit = rid == first
        keep = jnp.where(hit, 1.0, keep)
        grp = jnp.where(hit, NEG_INF, grp)

    cand = [jnp.where(keep[g:g + 1, :] > 0.5, bi_g[g], NEG_INF) for g in range(N_GROUPS)]
    eid_g = [rid + GROUP_SIZE * g for g in range(N_GROUPS)]
    onehot = [jnp.zeros((GROUP_SIZE, tr), F32) for _ in range(N_GROUPS)]
    e_sel, g_sel = [], []
    for _ in range(TOP_K):
        m = cand[0]
        for g in range(1, N_GROUPS):
            m = jnp.maximum(m, cand[g])
        m = _col_max(m)
        first = jnp.where(cand[0] == m, eid_g[0], N_EXPERTS)
        for g in range(1, N_GROUPS):
            first = jnp.minimum(first, jnp.where(cand[g] == m, eid_g[g], N_EXPERTS))
        first = _col_min(first)
        gate = jnp.zeros((GROUP_SIZE, tr), F32)
        for g in range(N_GROUPS):
            hit = eid_g[g] == first
            gate = gate + jnp.where(hit, sc_g[g], 0.0)
            cand[g] = jnp.where(hit, NEG_INF, cand[g])
            onehot[g] = jnp.where(hit, 1.0, onehot[g])
        e_sel.append(first)
        g_sel.append(jnp.sum(gate, axis=0, keepdims=True))

    gates = jnp.concatenate(g_sel, axis=0)
    gates = gates / (jnp.sum(gates, axis=0, keepdims=True) + 1e-20) * ROUTED_SCALE
    eidx = jnp.concatenate(e_sel, axis=0)
    eidx_ref[...] = eidx
    gate_ref[...] = gates

    oh = jnp.concatenate(onehot, axis=0)
    srow = lax.broadcasted_iota(I32, (tr, tr), 0)
    scol = lax.broadcasted_iota(I32, (tr, tr), 1)
    strict = (srow < scol).astype(BF16)
    before = jnp.dot(oh.astype(BF16), strict, preferred_element_type=F32) + carry[:, 0:1]
    ranks = []
    for k in range(TOP_K):
        acc = jnp.zeros((GROUP_SIZE, tr), F32)
        for g in range(N_GROUPS):
            acc = acc + jnp.where(eid_g[g] == e_sel[k], before[GROUP_SIZE * g:GROUP_SIZE * (g + 1)], 0.0)
        ranks.append(jnp.sum(acc, axis=0, keepdims=True))
    rank_ref[...] = jnp.concatenate(ranks, axis=0).astype(I32)
    carry[...] = carry[...] + jnp.sum(oh, axis=1, keepdims=True)
    cnt_ref[...] = carry[...].astype(I32)


def _router(logits_t, bias_b, *, tr):
    n_exp, t = logits_t.shape
    return pl.pallas_call(
        functools.partial(_router_kernel, tr=tr),
        grid=(t // tr,),
        in_specs=[
            pl.BlockSpec((n_exp, tr), lambda i: (0, i)),
            pl.BlockSpec((n_exp, tr), lambda i: (0, 0)),
        ],
        out_specs=[
            pl.BlockSpec((TOP_K, tr), lambda i: (0, i)),
            pl.BlockSpec((TOP_K, tr), lambda i: (0, i)),
            pl.BlockSpec((TOP_K, tr), lambda i: (0, i)),
            pl.BlockSpec((n_exp, V7X_LANES), lambda i: (0, 0)),
        ],
        out_shape=[
            jax.ShapeDtypeStruct((TOP_K, t), I32),
            jax.ShapeDtypeStruct((TOP_K, t), F32),
            jax.ShapeDtypeStruct((TOP_K, t), I32),
            jax.ShapeDtypeStruct((n_exp, V7X_LANES), I32),
        ],
        scratch_shapes=[pltpu.VMEM((n_exp, V7X_LANES), F32)],
        compiler_params=_params(("arbitrary",)),
        name="router_select",
    )(logits_t, bias_b)


def _dest_kernel(offs_ref, eidx_ref, rank_ref, dest_ref):
    eidx = eidx_ref[...]
    base = jnp.zeros(eidx.shape, I32)
    for e in range(N_EXPERTS):
        base = jnp.where(eidx == e, offs_ref[e], base)
    dest_ref[...] = base + rank_ref[...]


def _dest(offs, eidx_t, rank_t, *, tr):
    t = eidx_t.shape[1]
    return pl.pallas_call(
        _dest_kernel,
        grid_spec=pltpu.PrefetchScalarGridSpec(
            num_scalar_prefetch=1,
            grid=(t // tr,),
            in_specs=[pl.BlockSpec((TOP_K, tr), lambda i, o: (0, i)),
                      pl.BlockSpec((TOP_K, tr), lambda i, o: (0, i))],
            out_specs=pl.BlockSpec((TOP_K, tr), lambda i, o: (0, i)),
        ),
        out_shape=jax.ShapeDtypeStruct((TOP_K, t), I32),
        compiler_params=_params(("arbitrary",)),
        name="dispatch_rows",
    )(offs, eidx_t, rank_t)


def _dispatch_kernel(pad_end_ref, padded_ref, dest_ref, hp_ref, xs_ref, zero_scr, sem, *, td):
    i = pl.program_id(0)

    def row_copy(t, k):
        return pltpu.make_async_copy(hp_ref.at[pl.ds(t, 1)], xs_ref.at[pl.ds(dest_ref[k, t], 1)], sem)

    @pl.when(i == 0)
    def _():
        zero_scr[...] = jnp.zeros_like(zero_scr)

        def zero_copy(e):
            start = pl.multiple_of(pad_end_ref[e] - EXPERT_BLOCK, EXPERT_BLOCK)
            return pltpu.make_async_copy(zero_scr, xs_ref.at[pl.ds(start, EXPERT_BLOCK)], sem)

        def zstart(e, c):
            @pl.when(padded_ref[e] > 0)
            def _():
                zero_copy(e).start()
            return c

        def zwait(e, c):
            @pl.when(padded_ref[e] > 0)
            def _():
                zero_copy(e).wait()
            return c

        lax.fori_loop(0, N_EXPERTS, zstart, 0)
        lax.fori_loop(0, N_EXPERTS, zwait, 0)

    def start(t, c):
        for k in range(TOP_K):
            row_copy(t, k).start()
        return c

    def wait(t, c):
        for k in range(TOP_K):
            row_copy(t, k).wait()
        return c

    lax.fori_loop(0, td, start, 0)
    lax.fori_loop(0, td, wait, 0)


def _dispatch(pad_end, padded, dest_t, h2p, *, n_rows, td):
    t, half = h2p.shape
    return pl.pallas_call(
        functools.partial(_dispatch_kernel, td=td),
        grid_spec=pltpu.PrefetchScalarGridSpec(
            num_scalar_prefetch=2,
            grid=(t // td,),
            in_specs=[
                pl.BlockSpec((TOP_K, td), lambda i, a, b: (0, i), memory_space=pltpu.SMEM),
                pl.BlockSpec((td, half), lambda i, a, b: (i, 0)),
            ],
            out_specs=pl.BlockSpec(memory_space=pl.ANY),
            scratch_shapes=[pltpu.VMEM((EXPERT_BLOCK, half), U32), pltpu.SemaphoreType.DMA(())],
        ),
        out_shape=jax.ShapeDtypeStruct((n_rows, half), U32),
        compiler_params=_params(("arbitrary",)),
        name="dispatch_scatter",
    )(pad_end, padded, dest_t, h2p)


def _expert_kernel(be_ref, used_ref, xs_ref, wg_ref, wu_ref, wd_ref, ys_ref):
    n = pl.program_id(0)

    @pl.when(n < used_ref[0])
    def _():
        lo, hi = _unpack_bf16_pair(xs_ref[...])
        lo = lo.astype(BF16)
        hi = hi.astype(BF16)
        half = lo.shape[-1]
        g = (jnp.dot(lo, wg_ref[0, :half, :], preferred_element_type=F32)
             + jnp.dot(hi, wg_ref[0, half:, :], preferred_element_type=F32))
        u = (jnp.dot(lo, wu_ref[0, :half, :], preferred_element_type=F32)
             + jnp.dot(hi, wu_ref[0, half:, :], preferred_element_type=F32))
        hid = (_silu(g) * u).astype(BF16)
        y = jnp.dot(hid, wd_ref[0], preferred_element_type=F32)
        ys_ref[...] = _pack_bf16_pair(y[:, :half], y[:, half:])


def _experts(block_expert, used, xs, wg, wu, wd):
    n_rows, half = xs.shape
    n_blocks = n_rows // EXPERT_BLOCK
    _, d, hid = wg.shape
    blk = lambda n, be, used: (jnp.minimum(n, used[0] - 1), 0)
    wsel = lambda n, be, used: (be[jnp.minimum(n, used[0] - 1)], 0, 0)
    return pl.pallas_call(
        _expert_kernel,
        grid_spec=pltpu.PrefetchScalarGridSpec(
            num_scalar_prefetch=2,
            grid=(n_blocks,),
            in_specs=[
                pl.BlockSpec((EXPERT_BLOCK, half), blk),
                pl.BlockSpec((1, d, hid), wsel),
                pl.BlockSpec((1, d, hid), wsel),
                pl.BlockSpec((1, hid, d), wsel),
            ],
            out_specs=pl.BlockSpec((EXPERT_BLOCK, half), blk),
        ),
        out_shape=jax.ShapeDtypeStruct((n_rows, half), U32),
        compiler_params=_params(("arbitrary",)),
        name="expert_ffn",
    )(block_expert, used, xs, wg, wu, wd)


def _combine_kernel(dest_ref, x_ref, hp_ref, gate_ref, ys_ref, wsg_ref, wsu_ref, wsd_ref, g2_ref, fng_ref,
                    o_ref, gbuf, sem, *, td, final_norm):
    def row_copy(t, k):
        return pltpu.make_async_copy(ys_ref.at[pl.ds(dest_ref[k, t], 1)], gbuf.at[k, pl.ds(t, 1)], sem)

    def start(t, c):
        for k in range(TOP_K):
            row_copy(t, k).start()
        return c

    def wait(t, c):
        for k in range(TOP_K):
            row_copy(t, k).wait()
        return c

    lax.fori_loop(0, td, start, 0)

    lo, hi = _unpack_bf16_pair(hp_ref[...])
    lo = lo.astype(BF16)
    hi = hi.astype(BF16)
    half = lo.shape[-1]
    g = (jnp.dot(lo, wsg_ref[:half, :], preferred_element_type=F32)
         + jnp.dot(hi, wsg_ref[half:, :], preferred_element_type=F32))
    u = (jnp.dot(lo, wsu_ref[:half, :], preferred_element_type=F32)
         + jnp.dot(hi, wsu_ref[half:, :], preferred_element_type=F32))
    hid = (_silu(g) * u).astype(BF16)
    shared = jnp.dot(hid, wsd_ref[...], preferred_element_type=F32)

    lax.fori_loop(0, td, wait, 0)

    gate = gate_ref[...]
    r_lo = jnp.zeros((td, half), F32)
    r_hi = jnp.zeros((td, half), F32)
    for k in range(TOP_K):
        y_lo, y_hi = _unpack_bf16_pair(gbuf[k])
        w = gate[:, k:k + 1]
        r_lo = r_lo + w * y_lo
        r_hi = r_hi + w * y_hi
    routed = jnp.concatenate([r_lo, r_hi], axis=-1)
    out = x_ref[...] + g2_ref[0] * (routed + shared)
    if final_norm:
        out = _rms(out, fng_ref[...])
    o_ref[...] = out


def _combine(dest_t, x2, h2p, gate_rows, ys, wsg, wsu, wsd, mod, fng, *, seq, td, final_norm):
    t, d = x2.shape
    half = d // 2
    nb = seq // td
    full = lambda a: pl.BlockSpec(a.shape, lambda i: (0,) * a.ndim)
    return pl.pallas_call(
        functools.partial(_combine_kernel, td=td, final_norm=final_norm),
        grid=(t // td,),
        in_specs=[
            pl.BlockSpec((TOP_K, td), lambda i: (0, i), memory_space=pltpu.SMEM),
            pl.BlockSpec((td, d), lambda i: (i, 0)),
            pl.BlockSpec((td, half), lambda i: (i, 0)),
            pl.BlockSpec((td, TOP_K), lambda i: (i, 0)),
            pl.BlockSpec(memory_space=pl.ANY),
            full(wsg), full(wsu), full(wsd),
            pl.BlockSpec((1, 1, d), lambda i: (i // nb, 0, 5)),
            full(fng),
        ],
        out_specs=pl.BlockSpec((td, d), lambda i: (i, 0)),
        out_shape=jax.ShapeDtypeStruct((t, d), F32),
        scratch_shapes=[pltpu.VMEM((TOP_K, td, half), U32), pltpu.SemaphoreType.DMA(())],
        compiler_params=_params(("arbitrary",)),
        name="shared_combine",
    )(dest_t, x2, h2p, gate_rows, ys, wsg, wsu, wsd, mod, fng)


def kernel(x, c, w_ada, b_ada, norm1_g, norm2_g, w_in, b_forget, sc_conv_w, cf_conv_w, cf_conv_b,
           cf_ln_g, cf_ln_b, out_norm_g, w_out, w_router, router_bias, w_gate, w_up, w_down,
           ws_gate, ws_up, ws_down, final_norm_g):
    batch, seq, d = x.shape
    depth = w_ada.shape[0]
    t = batch * seq
    n_heads = b_forget.shape[1]
    attn_w = n_heads * HEAD_DIM
    sc_w = sc_conv_w.shape[2]
    cf_w = cf_conv_w.shape[2]
    conv_w = 3 * sc_w + 2 * cf_w
    assert w_in.shape[2] == 3 * attn_w + n_heads + conv_w
    assert batch <= V7X_SUBLANES and n_heads <= V7X_SUBLANES
    tl = _tiles(seq)
    n_assign = t * TOP_K
    n_blocks = -(-(n_assign + N_EXPERTS * (EXPERT_BLOCK - 1)) // EXPERT_BLOCK)
    n_rows = n_blocks * EXPERT_BLOCK

    c_pad = jnp.pad(c, ((0, V7X_SUBLANES - batch), (0, 0)))
    mod_all = _ada(c_pad, w_ada, b_ada)

    x2 = x.reshape(t, d)
    for l in range(depth):
        mod = mod_all[l, :batch].reshape(batch, 1, N_MOD * d)
        wl = w_in[l]
        w_main = jnp.concatenate([wl[:, :3 * attn_w], wl[:, 3 * attn_w + n_heads:]], axis=1).astype(BF16)
        w_fcols = wl[:, 3 * attn_w:3 * attn_w + n_heads]
        w_f = jnp.pad(w_fcols, ((0, 0), (0, V7X_LANES - n_heads))).astype(BF16)
        w_ft = jnp.pad(w_fcols.T, ((0, 2 * V7X_SUBLANES - n_heads), (0, 0))).astype(BF16)
        qkv, conv_in, f_rows, f_t = _inproj(
            x2, norm1_g[l].reshape(1, d), mod, w_main, w_f, w_ft,
            seq=seq, attn_width=attn_w, conv_width=conv_w, tm=tl["tm_in"], tn=tl["tn_in"])

        bias_row = jnp.pad(b_forget[l], (0, V7X_LANES - n_heads)).reshape(1, V7X_LANES)
        bias_col = jnp.broadcast_to(
            jnp.pad(b_forget[l], (0, 2 * V7X_SUBLANES - n_heads))[:, None], (2 * V7X_SUBLANES, V7X_LANES))
        cum_rows, cum_t = _fcum(f_rows, f_t, bias_row, bias_col, seq=seq, tc=tl["tc"])

        y_attn = _attention(qkv, cum_rows, cum_t, batch=batch, seq=seq, n_heads=n_heads, tq=tl["tq"])

        scw = jnp.pad(sc_conv_w[l], ((0, V7X_SUBLANES - SC_KERNEL), (0, 0)))
        cfw = jnp.pad(cf_conv_w[l], ((0, CF_HALO - CF_KERNEL), (0, 0)))
        wr_t = w_router[l].T
        wr_hi = wr_t.astype(BF16)
        wr_lo = (wr_t - wr_hi.astype(F32)).astype(BF16)
        x2, h2p, logits_t = _mixer_out(
            x2, y_attn, conv_in, scw, cfw, cf_conv_b[l].reshape(1, cf_w), cf_ln_g[l].reshape(1, cf_w),
            cf_ln_b[l].reshape(1, cf_w), out_norm_g[l].reshape(1, d), w_out[l].astype(BF16), mod,
            norm2_g[l].reshape(1, d), wr_hi, wr_lo, seq=seq, tm=tl["tm_mix"])

        bias_b = jnp.broadcast_to(router_bias[l][:, None], (N_EXPERTS, tl["tr"]))
        eidx_t, gate_t, rank_t, counts = _router(logits_t, bias_b, tr=tl["tr"])
        counts = counts[:, 0]
        padded = (counts + EXPERT_BLOCK - 1) // EXPERT_BLOCK * EXPERT_BLOCK
        pad_end = jnp.cumsum(padded).astype(I32)
        offs = pad_end - padded
        dest_t = _dest(offs, eidx_t, rank_t, tr=tl["tr"])
        block_expert = jnp.minimum(
            jnp.searchsorted(pad_end, jnp.arange(n_blocks, dtype=I32) * EXPERT_BLOCK, side="right"),
            N_EXPERTS - 1).astype(I32)
        used = (pad_end[-1:] // EXPERT_BLOCK).astype(I32)

        xs = _dispatch(pad_end, padded, dest_t, h2p, n_rows=n_rows, td=tl["td"])
        ys = _experts(block_expert, used, xs, w_gate[l].astype(BF16), w_up[l].astype(BF16),
                      w_down[l].astype(BF16))
        x2 = _combine(dest_t, x2, h2p, gate_t.T, ys, ws_gate[l].astype(BF16), ws_up[l].astype(BF16),
                      ws_down[l].astype(BF16), mod, final_norm_g.reshape(1, d),
                      seq=seq, td=tl["td"], final_norm=(l == depth - 1))
    return x2.reshape(batch, seq, d)
```

```python
import functools
import math

import jax
import jax.numpy as jnp
from jax import lax
from jax.experimental import pallas as pl
from jax.experimental.pallas import tpu as pltpu

F32 = jnp.float32
BF16 = jnp.bfloat16
I32 = jnp.int32
U32 = jnp.uint32

V7X_LANES = 128
V7X_SUBLANES = 8
V7X_VMEM_BYTES = 64 * 1024 * 1024
VMEM_LIMIT = V7X_VMEM_BYTES - 8 * 1024 * 1024

HEAD_DIM = 128
N_EXPERTS = 64
TOP_K = 8
N_GROUPS = 8
GROUP_SIZE = N_EXPERTS // N_GROUPS
TOPK_GROUPS = 4
ROUTED_SCALE = 2.5
EXPERT_BLOCK = 256
SC_KERNEL = 3
CF_KERNEL = 31
N_MOD = 6
NORM_EPS = 1e-6
SC_HALO = 8
CF_HALO = 32
NEG_INF = float("-inf")
LOG2E = math.log2(math.e)
F_ROWS = 16
ATTN_CHAIN = 256


def _params(semantics, **kw):
    return pltpu.CompilerParams(dimension_semantics=semantics, vmem_limit_bytes=VMEM_LIMIT, **kw)


def _tiles(seq):
    return dict(
        tm_in=min(1024, seq),
        tn_in=512,
        tc=min(512, seq),
        tq=min(512, seq),
        tm_mix=min(256, seq),
        tr=min(512, seq),
        td=min(256, seq),
    )


def _rms(x, g):
    return x * lax.rsqrt(jnp.mean(x * x, axis=-1, keepdims=True) + NORM_EPS) * g


def _silu(x):
    return x * jax.nn.sigmoid(x)


def _pack_bf16_pair(lo, hi):
    lo_b = lax.bitcast_convert_type(lo.astype(BF16).astype(F32), U32) >> 16
    hi_b = lax.bitcast_convert_type(hi.astype(BF16).astype(F32), U32) & jnp.uint32(0xFFFF0000)
    return hi_b | lo_b


def _unpack_bf16_pair(p):
    lo = lax.bitcast_convert_type(p << 16, F32)
    hi = lax.bitcast_convert_type(p & jnp.uint32(0xFFFF0000), F32)
    return lo, hi


def _split3(x):
    a = x.astype(BF16)
    r = x - a.astype(F32)
    b = r.astype(BF16)
    c = (r - b.astype(F32)).astype(BF16)
    return a, b, c


def _dot_nt(a, b):
    return lax.dot_general(a, b, (((1,), (1,)), ((), ())), preferred_element_type=F32)


def _ada_kernel(c_ref, w_ref, b_ref, o_ref):
    c = c_ref[...]
    ca = _silu(c).astype(BF16)
    o_ref[0] = jnp.dot(ca, w_ref[0].astype(BF16), preferred_element_type=F32) + b_ref[0]


def _ada(c_pad, w_ada, b_ada):
    depth, d, n = w_ada.shape
    tn = 1024
    return pl.pallas_call(
        _ada_kernel,
        grid=(depth, n // tn),
        in_specs=[
            pl.BlockSpec((V7X_SUBLANES, d), lambda l, j: (0, 0)),
            pl.BlockSpec((1, d, tn), lambda l, j: (l, 0, j)),
            pl.BlockSpec((1, 1, tn), lambda l, j: (l, 0, j)),
        ],
        out_specs=pl.BlockSpec((1, V7X_SUBLANES, tn), lambda l, j: (l, 0, j)),
        out_shape=jax.ShapeDtypeStruct((depth, V7X_SUBLANES, n), F32),
        compiler_params=_params(("arbitrary", "arbitrary")),
        name="ada_mod",
    )(c_pad, w_ada, b_ada.reshape(depth, 1, n))


def _inproj_kernel(x_ref, g_ref, sh_ref, sc_ref, wt_ref, w_ref, wf_ref, wft_ref,
                   qvt_ref, k_ref, conv_ref, fr_ref, ft_ref, h_scr,
                   *, n_q_steps, n_t_steps, n_k_steps, heads_per_step):
    j = pl.program_id(1)

    @pl.when(j == 0)
    def _():
        h = _rms(x_ref[...], g_ref[...]) * (1.0 + sc_ref[0]) + sh_ref[0]
        hb = h.astype(BF16)
        h_scr[...] = hb
        fr_ref[...] = jnp.dot(hb, wf_ref[...], preferred_element_type=F32)
        ft_ref[...] = _dot_nt(wft_ref[...], hb)

    @pl.when(j < n_t_steps)
    def _():
        scale = jnp.where(j < n_q_steps, LOG2E * HEAD_DIM ** -0.5, 1.0).astype(F32)
        a = _dot_nt(wt_ref[...], h_scr[...]) * scale
        for u in range(heads_per_step):
            qvt_ref[u] = a[u * HEAD_DIM:(u + 1) * HEAD_DIM, :].astype(BF16)

    @pl.when(j >= n_t_steps)
    def _():
        a = jnp.dot(h_scr[...], w_ref[...], preferred_element_type=F32)

        @pl.when(j < n_t_steps + n_k_steps)
        def _():
            for u in range(heads_per_step):
                k_ref[u] = a[:, u * HEAD_DIM:(u + 1) * HEAD_DIM].astype(BF16)

        @pl.when(j >= n_t_steps + n_k_steps)
        def _():
            conv_ref[...] = a.astype(BF16)


def _inproj(x2, g1, mod, w_t, w_n, w_f, w_ft, *, seq, attn_width, conv_width, tm, tn):
    t, d = x2.shape
    nb = seq // tm
    n_q_steps = attn_width // tn
    n_t_steps = 2 * attn_width // tn
    n_k_steps = attn_width // tn
    n_c_steps = conv_width // tn
    n_steps = n_t_steps + n_k_steps + n_c_steps
    hps = tn // HEAD_DIM
    n_heads = attn_width // HEAD_DIM
    kern = functools.partial(_inproj_kernel, n_q_steps=n_q_steps, n_t_steps=n_t_steps,
                             n_k_steps=n_k_steps, heads_per_step=hps)
    return pl.pallas_call(
        kern,
        grid=(t // tm, n_steps),
        in_specs=[
            pl.BlockSpec((tm, d), lambda i, j: (i, 0)),
            pl.BlockSpec((1, d), lambda i, j: (0, 0)),
            pl.BlockSpec((1, 1, d), lambda i, j: (i // nb, 0, 0)),
            pl.BlockSpec((1, 1, d), lambda i, j: (i // nb, 0, 1)),
            pl.BlockSpec((tn, d), lambda i, j: (jnp.minimum(j, n_t_steps - 1), 0)),
            pl.BlockSpec((d, tn), lambda i, j: (0, jnp.maximum(j - n_t_steps, 0))),
            pl.BlockSpec((d, V7X_LANES), lambda i, j: (0, 0)),
            pl.BlockSpec((2 * V7X_SUBLANES, d), lambda i, j: (0, 0)),
        ],
        out_specs=[
            pl.BlockSpec((hps, HEAD_DIM, tm), lambda i, j: (jnp.minimum(j, n_t_steps - 1), 0, i)),
            pl.BlockSpec((hps, tm, HEAD_DIM),
                         lambda i, j: (jnp.clip(j - n_t_steps, 0, n_k_steps - 1), i, 0)),
            pl.BlockSpec((tm, tn),
                         lambda i, j: (i, jnp.clip(j - n_t_steps - n_k_steps, 0, n_c_steps - 1))),
            pl.BlockSpec((tm, V7X_LANES), lambda i, j: (i, 0)),
            pl.BlockSpec((2 * V7X_SUBLANES, tm), lambda i, j: (0, i)),
        ],
        out_shape=[
            jax.ShapeDtypeStruct((2 * n_heads, HEAD_DIM, t), BF16),
            jax.ShapeDtypeStruct((n_heads, t, HEAD_DIM), BF16),
            jax.ShapeDtypeStruct((t, conv_width), BF16),
            jax.ShapeDtypeStruct((t, V7X_LANES), F32),
            jax.ShapeDtypeStruct((2 * V7X_SUBLANES, t), F32),
        ],
        scratch_shapes=[pltpu.VMEM((tm, d), BF16)],
        compiler_params=_params(("arbitrary", "arbitrary")),
        name="norm_inproj",
    )(x2, g1, mod, mod, w_t, w_n, w_f, w_ft)


def _log_sigmoid(z):
    return jnp.minimum(z, 0.0) - jnp.log1p(jnp.exp(-jnp.abs(z)))


def _fcum_kernel(fr_ref, ft_ref, br_ref, bt_ref, kf_ref, qf_ref, car_r, car_t, *, nb, tc, n_heads):
    i = pl.program_id(0)

    @pl.when(i % nb == 0)
    def _():
        car_r[...] = jnp.zeros_like(car_r)
        car_t[...] = jnp.zeros_like(car_t)

    row = lax.broadcasted_iota(I32, (tc, tc), 0)
    col = lax.broadcasted_iota(I32, (tc, tc), 1)
    lower = (row >= col).astype(BF16)
    upper = (row <= col).astype(BF16)

    lf_r = _log_sigmoid(fr_ref[...] + br_ref[...])
    cum_r = car_r[0:1, :]
    for part in _split3(lf_r):
        cum_r = cum_r + jnp.dot(lower, part, preferred_element_type=F32)
    car_r[...] = jnp.broadcast_to(cum_r[tc - 1:tc, :], car_r.shape)

    lf_t = _log_sigmoid(ft_ref[...] + bt_ref[:, 0:1])
    cum_t = car_t[:, 0:1]
    for part in _split3(lf_t):
        cum_t = cum_t + jnp.dot(part, upper, preferred_element_type=F32)
    car_t[...] = jnp.broadcast_to(cum_t[:, tc - 1:tc], car_t.shape)

    lane = lax.broadcasted_iota(I32, (tc, V7X_LANES), 1)
    sub = lax.broadcasted_iota(I32, (F_ROWS, tc), 0)
    for h in range(n_heads):
        k_hi, k_mid, k_lo = _split3(jnp.broadcast_to(cum_r[:, h:h + 1] * LOG2E, (tc, V7X_LANES)))
        ones = jnp.where(lane < 6, 1.0, 0.0).astype(BF16)
        kf_ref[h] = jnp.where(lane == 0, k_hi, jnp.where(lane == 1, k_mid, jnp.where(lane == 2, k_lo, ones)))
        q_hi, q_mid, q_lo = _split3(jnp.broadcast_to(cum_t[h:h + 1, :] * LOG2E, (F_ROWS, tc)))
        neg = jnp.where(sub < 3, -1.0, 0.0).astype(BF16)
        qf_ref[h] = jnp.where(sub == 3, q_hi, jnp.where(sub == 4, q_mid, jnp.where(sub == 5, q_lo, neg)))


def _fcum(f_rows, f_t, bias_row, bias_col, *, seq, tc, n_heads):
    t = f_rows.shape[0]
    nb = seq // tc
    rows_t = f_t.shape[0]
    return pl.pallas_call(
        functools.partial(_fcum_kernel, nb=nb, tc=tc, n_heads=n_heads),
        grid=(t // tc,),
        in_specs=[
            pl.BlockSpec((tc, V7X_LANES), lambda i: (i, 0)),
            pl.BlockSpec((rows_t, tc), lambda i: (0, i)),
            pl.BlockSpec((1, V7X_LANES), lambda i: (0, 0)),
            pl.BlockSpec((rows_t, V7X_LANES), lambda i: (0, 0)),
        ],
        out_specs=[
            pl.BlockSpec((n_heads, tc, V7X_LANES), lambda i: (0, i, 0)),
            pl.BlockSpec((n_heads, F_ROWS, tc), lambda i: (0, 0, i)),
        ],
        out_shape=[
            jax.ShapeDtypeStruct((n_heads, t, V7X_LANES), BF16),
            jax.ShapeDtypeStruct((n_heads, F_ROWS, t), BF16),
        ],
        scratch_shapes=[pltpu.VMEM((V7X_SUBLANES, V7X_LANES), F32),
                        pltpu.VMEM((rows_t, V7X_LANES), F32)],
        compiler_params=_params(("arbitrary",)),
        name="forget_cumsum",
    )(f_rows, f_t, bias_row, bias_col)


def _attn_kernel(q_ref, qf_ref, k_ref, kf_ref, vt_ref, o_ref, qa_scr, acc_scr, *, tq):
    i = pl.program_id(2)
    n_chain = tq // ATTN_CHAIN
    qa_scr[0:HEAD_DIM, :] = q_ref[0]
    qa_scr[HEAD_DIM:HEAD_DIM + F_ROWS, :] = qf_ref[0]
    qa_scr[HEAD_DIM + F_ROWS:, :] = jnp.zeros((HEAD_DIM - F_ROWS, tq), BF16)
    acc_scr[...] = jnp.zeros_like(acc_scr)

    def step(j, carry, masked):
        m_prev, l_prev = carry
        start = pl.multiple_of(j * tq, tq)
        kk = jnp.concatenate([k_ref[0, pl.ds(start, tq), :], kf_ref[0, pl.ds(start, tq), :]], axis=1)
        vt = vt_ref[0, :, pl.ds(start, tq)]
        m_out, l_out = [], []
        for c in range(n_chain):
            sl = slice(c * ATTN_CHAIN, (c + 1) * ATTN_CHAIN)
            s = jnp.dot(kk, qa_scr[:, sl], preferred_element_type=F32)
            if masked:
                key = lax.broadcasted_iota(I32, (tq, ATTN_CHAIN), 0)
                qry = lax.broadcasted_iota(I32, (tq, ATTN_CHAIN), 1) + c * ATTN_CHAIN
                s = jnp.where(key <= qry, s, NEG_INF)
            m_new = jnp.maximum(m_prev[:, sl], jnp.max(s, axis=0, keepdims=True))
            alpha = jnp.exp2(m_prev[:, sl] - m_new)
            p = jnp.exp2(s - m_new)
            l_out.append(alpha * l_prev[:, sl] + jnp.sum(p, axis=0, keepdims=True))
            acc_scr[:, sl] = alpha * acc_scr[:, sl] + jnp.dot(vt, p.astype(BF16),
                                                             preferred_element_type=F32)
            m_out.append(m_new)
        return jnp.concatenate(m_out, axis=1), jnp.concatenate(l_out, axis=1)

    init = (jnp.full((1, tq), NEG_INF, F32), jnp.zeros((1, tq), F32))
    carry = lax.fori_loop(0, i, lambda j, c: step(j, c, False), init)
    _, l_fin = step(i, carry, True)
    o_ref[...] = (acc_scr[...] / l_fin).T


def _attention(qvt, qf, k, kf, *, batch, seq, n_heads, tq):
    t = batch * seq
    nq = seq // tq
    return pl.pallas_call(
        functools.partial(_attn_kernel, tq=tq),
        grid=(batch, n_heads, nq),
        in_specs=[
            pl.BlockSpec((1, HEAD_DIM, tq), lambda b, h, i: (h, 0, b * nq + i)),
            pl.BlockSpec((1, F_ROWS, tq), lambda b, h, i: (h, 0, b * nq + i)),
            pl.BlockSpec((1, seq, HEAD_DIM), lambda b, h, i: (h, b, 0)),
            pl.BlockSpec((1, seq, V7X_LANES), lambda b, h, i: (h, b, 0)),
            pl.BlockSpec((1, HEAD_DIM, seq), lambda b, h, i: (n_heads + h, 0, b)),
        ],
        out_specs=pl.BlockSpec((tq, HEAD_DIM), lambda b, h, i: (b * nq + i, h)),
        out_shape=jax.ShapeDtypeStruct((t, n_heads * HEAD_DIM), F32),
        scratch_shapes=[pltpu.VMEM((2 * HEAD_DIM, tq), BF16), pltpu.VMEM((HEAD_DIM, tq), F32)],
        compiler_params=_params(("arbitrary", "arbitrary", "arbitrary")),
        name="forget_attention",
    )(qvt, qf, k, kf, qvt)


def _mixer_out_kernel(x_ref, ya_ref, cv_ref, scw_ref, cfw_ref, cfb_ref, lng_ref, lnb_ref, ong_ref,
                      wout_ref, g1_ref, n2g_ref, sh2_ref, sc2_ref, wr_hi_ref, wr_lo_ref,
                      xo_ref, hp_ref, lg_ref, zbuf, ubuf, *, nb, tm, sc_w, cf_w, attn_w):
    i = pl.program_id(0)

    @pl.when(i % nb == 0)
    def _():
        zbuf[0:SC_HALO, :] = jnp.zeros((SC_HALO, sc_w), F32)
        ubuf[0:CF_HALO, :] = jnp.zeros((CF_HALO, cf_w), F32)

    cv = cv_ref[...].astype(F32)
    sc_b = cv[:, 0:sc_w]
    sc_c = cv[:, sc_w:2 * sc_w]
    sc_u = cv[:, 2 * sc_w:3 * sc_w]
    cf_a = cv[:, 3 * sc_w:3 * sc_w + cf_w]
    cf_gate = cv[:, 3 * sc_w + cf_w:3 * sc_w + 2 * cf_w]

    zbuf[SC_HALO:SC_HALO + tm, :] = sc_c * sc_u
    conv = jnp.zeros((tm, sc_w), F32)
    for k in range(SC_KERNEL):
        off = SC_HALO - (SC_KERNEL - 1) + k
        conv = conv + scw_ref[k:k + 1, :] * zbuf[off:off + tm, :]
    y_sc = sc_b * conv
    zbuf[0:SC_HALO, :] = zbuf[tm:tm + SC_HALO, :]

    ubuf[CF_HALO:CF_HALO + tm, :] = cf_a * jax.nn.sigmoid(cf_gate)
    conv = jnp.zeros((tm, cf_w), F32) + cfb_ref[...]
    for k in range(CF_KERNEL):
        off = CF_HALO - (CF_KERNEL - 1) + k
        conv = conv + cfw_ref[k:k + 1, :] * ubuf[off:off + tm, :]
    ubuf[0:CF_HALO, :] = ubuf[tm:tm + CF_HALO, :]
    mu = jnp.mean(conv, axis=-1, keepdims=True)
    cen = conv - mu
    var = jnp.mean(cen * cen, axis=-1, keepdims=True)
    y_cf = _silu(cen * lax.rsqrt(var + NORM_EPS) * lng_ref[...] + lnb_ref[...])

    ong = ong_ref[...]
    y = jnp.concatenate([
        _rms(ya_ref[...], ong[:, 0:attn_w]),
        _rms(y_sc, ong[:, attn_w:attn_w + sc_w]),
        _rms(y_cf, ong[:, attn_w + sc_w:attn_w + sc_w + cf_w]),
    ], axis=-1).astype(BF16)
    mix = jnp.dot(y, wout_ref[...], preferred_element_type=F32)
    x_new = x_ref[...] + g1_ref[0] * mix
    xo_ref[...] = x_new

    h2 = _rms(x_new, n2g_ref[...]) * (1.0 + sc2_ref[0]) + sh2_ref[0]
    half = h2.shape[-1] // 2
    hp_ref[...] = _pack_bf16_pair(h2[:, :half], h2[:, half:])
    h_hi = h2.astype(BF16)
    h_lo = (h2 - h_hi.astype(F32)).astype(BF16)
    w_hi = wr_hi_ref[...]
    lg_ref[...] = _dot_nt(w_hi, h_hi) + _dot_nt(w_hi, h_lo) + _dot_nt(wr_lo_ref[...], h_hi)


def _mixer_out(x2, y_attn, conv_in, scw, cfw, cfb, lng, lnb, ong, w_out, mod, n2g, wr_hi, wr_lo,
               *, seq, tm):
    t, d = x2.shape
    attn_w = y_attn.shape[1]
    sc_w = scw.shape[1]
    cf_w = cfw.shape[1]
    conv_w = conv_in.shape[1]
    nb = seq // tm
    n_exp = wr_hi.shape[0]
    full = lambda a: pl.BlockSpec(a.shape, lambda i: (0,) * a.ndim)
    modspec = lambda c: pl.BlockSpec((1, 1, d), lambda i: (i // nb, 0, c))
    kern = functools.partial(_mixer_out_kernel, nb=nb, tm=tm, sc_w=sc_w, cf_w=cf_w, attn_w=attn_w)
    return pl.pallas_call(
        kern,
        grid=(t // tm,),
        in_specs=[
            pl.BlockSpec((tm, d), lambda i: (i, 0)),
            pl.BlockSpec((tm, attn_w), lambda i: (i, 0)),
            pl.BlockSpec((tm, conv_w), lambda i: (i, 0)),
            full(scw), full(cfw), full(cfb), full(lng), full(lnb), full(ong), full(w_out),
            modspec(2), full(n2g), modspec(3), modspec(4), full(wr_hi), full(wr_lo),
        ],
        out_specs=[
            pl.BlockSpec((tm, d), lambda i: (i, 0)),
            pl.BlockSpec((tm, d // 2), lambda i: (i, 0)),
            pl.BlockSpec((n_exp, tm), lambda i: (0, i)),
        ],
        out_shape=[
            jax.ShapeDtypeStruct((t, d), F32),
            jax.ShapeDtypeStruct((t, d // 2), U32),
            jax.ShapeDtypeStruct((n_exp, t), F32),
        ],
        scratch_shapes=[pltpu.VMEM((SC_HALO + tm, sc_w), F32), pltpu.VMEM((CF_HALO + tm, cf_w), F32)],
        compiler_params=_params(("arbitrary",)),
        name="conv_norm_outproj",
    )(x2, y_attn, conv_in, scw, cfw, cfb, lng, lnb, ong, w_out, mod, n2g, mod, mod, wr_hi, wr_lo)


def _col_max(x):
    return jnp.max(x, axis=0, keepdims=True)


def _col_min(x):
    return jnp.min(x, axis=0, keepdims=True)


def _router_kernel(lg_ref, bias_ref, eidx_ref, gate_ref, rank_ref, cnt_ref, carry, *, tr):
    @pl.when(pl.program_id(0) == 0)
    def _():
        carry[...] = jnp.zeros_like(carry)

    scores = jax.nn.sigmoid(lg_ref[...])
    biased = scores + bias_ref[...]
    rid = lax.broadcasted_iota(I32, (GROUP_SIZE, tr), 0)
    sc_g = [scores[GROUP_SIZE * g:GROUP_SIZE * (g + 1)] for g in range(N_GROUPS)]
    bi_g = [biased[GROUP_SIZE * g:GROUP_SIZE * (g + 1)] for g in range(N_GROUPS)]

    gs = []
    for g in range(N_GROUPS):
        m1 = _col_max(bi_g[g])
        first = _col_min(jnp.where(bi_g[g] == m1, rid, GROUP_SIZE))
        m2 = _col_max(jnp.where(rid == first, NEG_INF, bi_g[g]))
        gs.append(m1 + m2)
    grp = jnp.concatenate(gs, axis=0)

    keep = jnp.zeros((N_GROUPS, tr), F32)
    for _ in range(TOPK_GROUPS):
        m = _col_max(grp)
        first = _col_min(jnp.where(grp == m, rid, N_GROUPS))
        hit = rid == first
        keep = jnp.where(hit, 1.0, keep)
        grp = jnp.where(hit, NEG_INF, grp)

    cand = [jnp.where(keep[g:g + 1, :] > 0.5, bi_g[g], NEG_INF) for g in range(N_GROUPS)]
    eid_g = [rid + GROUP_SIZE * g for g in range(N_GROUPS)]
    onehot = [jnp.zeros((GROUP_SIZE, tr), F32) for _ in range(N_GROUPS)]
    e_sel, g_sel = [], []
    for _ in range(TOP_K):
        m = cand[0]
        for g in range(1, N_GROUPS):
            m = jnp.maximum(m, cand[g])
        m = _col_max(m)
        first = jnp.where(cand[0] == m, eid_g[0], N_EXPERTS)
        for g in range(1, N_GROUPS):
            first = jnp.minimum(first, jnp.where(cand[g] == m, eid_g[g], N_EXPERTS))
        first = _col_min(first)
        gate = jnp.zeros((GROUP_SIZE, tr), F32)
        for g in range(N_GROUPS):
            hit = eid_g[g] == first
            gate = gate + jnp.where(hit, sc_g[g], 0.0)
            cand[g] = jnp.where(hit, NEG_INF, cand[g])
            onehot[g] = jnp.where(hit, 1.0, onehot[g])
        e_sel.append(first)
        g_sel.append(jnp.sum(gate, axis=0, keepdims=True))

    gates = jnp.concatenate(g_sel, axis=0)
    gates = gates / (jnp.sum(gates, axis=0, keepdims=True) + 1e-20) * ROUTED_SCALE
    eidx = jnp.concatenate(e_sel, axis=0)
    eidx_ref[...] = eidx
    gate_ref[...] = gates

    oh = jnp.concatenate(onehot, axis=0)
    srow = lax.broadcasted_iota(I32, (tr, tr), 0)
    scol = lax.broadcasted_iota(I32, (tr, tr), 1)
    strict = (srow < scol).astype(BF16)
    before = jnp.dot(oh.astype(BF16), strict, preferred_element_type=F32) + carry[:, 0:1]
    ranks = []
    for k in range(TOP_K):
        acc = jnp.zeros((GROUP_SIZE, tr), F32)
        for g in range(N_GROUPS):
            acc = acc + jnp.where(eid_g[g] == e_sel[k], before[GROUP_SIZE * g:GROUP_SIZE * (g + 1)], 0.0)
        ranks.append(jnp.sum(acc, axis=0, keepdims=True))
    rank_ref[...] = jnp.concatenate(ranks, axis=0).astype(I32)
    carry[...] = carry[...] + jnp.sum(oh, axis=1, keepdims=True)
    cnt_ref[...] = carry[...].astype(I32)


def _router(logits_t, bias_b, *, tr):
    n_exp, t = logits_t.shape
    return pl.pallas_call(
        functools.partial(_router_kernel, tr=tr),
        grid=(t // tr,),
        in_specs=[
            pl.BlockSpec((n_exp, tr), lambda i: (0, i)),
            pl.BlockSpec((n_exp, tr), lambda i: (0, 0)),
        ],
        out_specs=[
            pl.BlockSpec((TOP_K, tr), lambda i: (0, i)),
            pl.BlockSpec((TOP_K, tr), lambda i: (0, i)),
            pl.BlockSpec((TOP_K, tr), lambda i: (0, i)),
            pl.BlockSpec((n_exp, V7X_LANES), lambda i: (0, 0)),
        ],
        out_shape=[
            jax.ShapeDtypeStruct((TOP_K, t), I32),
            jax.ShapeDtypeStruct((TOP_K, t), F32),
            jax.ShapeDtypeStruct((TOP_K, t), I32),
            jax.ShapeDtypeStruct((n_exp, V7X_LANES), I32),
        ],
        scratch_shapes=[pltpu.VMEM((n_exp, V7X_LANES), F32)],
        compiler_params=_params(("arbitrary",)),
        name="router_select",
    )(logits_t, bias_b)


def _dest_kernel(offs_ref, eidx_ref, rank_ref, dest_ref):
    eidx = eidx_ref[...]
    base = jnp.zeros(eidx.shape, I32)
    for e in range(N_EXPERTS):
        base = jnp.where(eidx == e, offs_ref[e], base)
    dest_ref[...] = base + rank_ref[...]


def _dest(offs, eidx_t, rank_t, *, tr):
    t = eidx_t.shape[1]
    return pl.pallas_call(
        _dest_kernel,
        grid_spec=pltpu.PrefetchScalarGridSpec(
            num_scalar_prefetch=1,
            grid=(t // tr,),
            in_specs=[pl.BlockSpec((TOP_K, tr), lambda i, o: (0, i)),
                      pl.BlockSpec((TOP_K, tr), lambda i, o: (0, i))],
            out_specs=pl.BlockSpec((TOP_K, tr), lambda i, o: (0, i)),
        ),
        out_shape=jax.ShapeDtypeStruct((TOP_K, t), I32),
        compiler_params=_params(("arbitrary",)),
        name="dispatch_rows",
    )(offs, eidx_t, rank_t)


def _dispatch_kernel(pad_end_ref, padded_ref, dest_ref, hp_ref, xs_ref, zero_scr, sem, *, td):
    i = pl.program_id(0)

    def row_copy(t, k):
        return pltpu.make_async_copy(hp_ref.at[pl.ds(t, 1)], xs_ref.at[pl.ds(dest_ref[k, t], 1)], sem)

    @pl.when(i == 0)
    def _():
        zero_scr[...] = jnp.zeros_like(zero_scr)

        def zero_copy(e):
            start = pl.multiple_of(pad_end_ref[e] - EXPERT_BLOCK, EXPERT_BLOCK)
            return pltpu.make_async_copy(zero_scr, xs_ref.at[pl.ds(start, EXPERT_BLOCK)], sem)

        def zstart(e, c):
            @pl.when(padded_ref[e] > 0)
            def _():
                zero_copy(e).start()
            return c

        def zwait(e, c):
            @pl.when(padded_ref[e] > 0)
            def _():
                zero_copy(e).wait()
            return c

        lax.fori_loop(0, N_EXPERTS, zstart, 0)
        lax.fori_loop(0, N_EXPERTS, zwait, 0)

    def start(t, c):
        for k in range(TOP_K):
            row_copy(t, k).start()
        return c

    def wait(t, c):
        for k in range(TOP_K):
            row_copy(t, k).wait()
        return c

    lax.fori_loop(0, td, start, 0)
    lax.fori_loop(0, td, wait, 0)


def _dispatch(pad_end, padded, dest_t, h2p, *, n_rows, td):
    t, half = h2p.shape
    return pl.pallas_call(
        functools.partial(_dispatch_kernel, td=td),
        grid_spec=pltpu.PrefetchScalarGridSpec(
            num_scalar_prefetch=2,
            grid=(t // td,),
            in_specs=[
                pl.BlockSpec((TOP_K, td), lambda i, a, b: (0, i), memory_space=pltpu.SMEM),
                pl.BlockSpec((td, half), lambda i, a, b: (i, 0)),
            ],
            out_specs=pl.BlockSpec(memory_space=pl.ANY),
            scratch_shapes=[pltpu.VMEM((EXPERT_BLOCK, half), U32), pltpu.SemaphoreType.DMA(())],
        ),
        out_shape=jax.ShapeDtypeStruct((n_rows, half), U32),
        compiler_params=_params(("arbitrary",)),
        name="dispatch_scatter",
    )(pad_end, padded, dest_t, h2p)


def _expert_kernel(be_ref, used_ref, xs_ref, wg_ref, wu_ref, wd_ref, ys_ref, wg_s, wu_s, wd_s):
    n = pl.program_id(0)

    @pl.when((n < used_ref[0]) & ((n == 0) | (be_ref[n] != be_ref[jnp.maximum(n - 1, 0)])))
    def _():
        wg_s[...] = wg_ref[0].astype(BF16)
        wu_s[...] = wu_ref[0].astype(BF16)
        wd_s[...] = wd_ref[0].astype(BF16)

    @pl.when(n < used_ref[0])
    def _():
        lo, hi = _unpack_bf16_pair(xs_ref[...])
        lo = lo.astype(BF16)
        hi = hi.astype(BF16)
        half = lo.shape[-1]
        g = (jnp.dot(lo, wg_s[:half, :], preferred_element_type=F32)
             + jnp.dot(hi, wg_s[half:, :], preferred_element_type=F32))
        u = (jnp.dot(lo, wu_s[:half, :], preferred_element_type=F32)
             + jnp.dot(hi, wu_s[half:, :], preferred_element_type=F32))
        hid = (_silu(g) * u).astype(BF16)
        y = jnp.dot(hid, wd_s[...], preferred_element_type=F32)
        ys_ref[...] = _pack_bf16_pair(y[:, :half], y[:, half:])


def _experts(block_expert, used, xs, wg, wu, wd, *, layer):
    n_rows, half = xs.shape
    n_blocks = n_rows // EXPERT_BLOCK
    _, _, d, hid = wg.shape
    blk = lambda n, be, used: (jnp.minimum(n, used[0] - 1), 0)
    wsel = lambda n, be, used: (layer, be[jnp.minimum(n, used[0] - 1)], 0, 0)
    return pl.pallas_call(
        _expert_kernel,
        grid_spec=pltpu.PrefetchScalarGridSpec(
            num_scalar_prefetch=2,
            grid=(n_blocks,),
            in_specs=[
                pl.BlockSpec((EXPERT_BLOCK, half), blk),
                pl.BlockSpec((None, 1, d, hid), wsel),
                pl.BlockSpec((None, 1, d, hid), wsel),
                pl.BlockSpec((None, 1, hid, d), wsel),
            ],
            out_specs=pl.BlockSpec((EXPERT_BLOCK, half), blk),
            scratch_shapes=[pltpu.VMEM((d, hid), BF16), pltpu.VMEM((d, hid), BF16),
                            pltpu.VMEM((hid, d), BF16)],
        ),
        out_shape=jax.ShapeDtypeStruct((n_rows, half), U32),
        compiler_params=_params(("arbitrary",)),
        name="expert_ffn",
    )(block_expert, used, xs, wg, wu, wd)


def _combine_kernel(dest_ref, x_ref, hp_ref, gate_ref, ys_ref, wsg_ref, wsu_ref, wsd_ref, g2_ref, fng_ref,
                    o_ref, gbuf, sem, *, td, final_norm):
    def row_copy(t, k):
        return pltpu.make_async_copy(ys_ref.at[pl.ds(dest_ref[k, t], 1)], gbuf.at[k, pl.ds(t, 1)], sem)

    def start(t, c):
        for k in range(TOP_K):
            row_copy(t, k).start()
        return c

    def wait(t, c):
        for k in range(TOP_K):
            row_copy(t, k).wait()
        return c

    lax.fori_loop(0, td, start, 0)

    lo, hi = _unpack_bf16_pair(hp_ref[...])
    lo = lo.astype(BF16)
    hi = hi.astype(BF16)
    half = lo.shape[-1]
    g = (jnp.dot(lo, wsg_ref[:half, :], preferred_element_type=F32)
         + jnp.dot(hi, wsg_ref[half:, :], preferred_element_type=F32))
    u = (jnp.dot(lo, wsu_ref[:half, :], preferred_element_type=F32)
         + jnp.dot(hi, wsu_ref[half:, :], preferred_element_type=F32))
    hid = (_silu(g) * u).astype(BF16)
    shared = jnp.dot(hid, wsd_ref[...], preferred_element_type=F32)

    lax.fori_loop(0, td, wait, 0)

    gate = gate_ref[...]
    r_lo = jnp.zeros((td, half), F32)
    r_hi = jnp.zeros((td, half), F32)
    for k in range(TOP_K):
        y_lo, y_hi = _unpack_bf16_pair(gbuf[k])
        w = gate[:, k:k + 1]
        r_lo = r_lo + w * y_lo
        r_hi = r_hi + w * y_hi
    routed = jnp.concatenate([r_lo, r_hi], axis=-1)
    out = x_ref[...] + g2_ref[0] * (routed + shared)
    if final_norm:
        out = _rms(out, fng_ref[...])
    o_ref[...] = out


def _combine(dest_t, x2, h2p, gate_rows, ys, wsg, wsu, wsd, mod, fng, *, seq, td, final_norm):
    t, d = x2.shape
    half = d // 2
    nb = seq // td
    full = lambda a: pl.BlockSpec(a.shape, lambda i: (0,) * a.ndim)
    return pl.pallas_call(
        functools.partial(_combine_kernel, td=td, final_norm=final_norm),
        grid=(t // td,),
        in_specs=[
            pl.BlockSpec((TOP_K, td), lambda i: (0, i), memory_space=pltpu.SMEM),
            pl.BlockSpec((td, d), lambda i: (i, 0)),
            pl.BlockSpec((td, half), lambda i: (i, 0)),
            pl.BlockSpec((td, TOP_K), lambda i: (i, 0)),
            pl.BlockSpec(memory_space=pl.ANY),
            full(wsg), full(wsu), full(wsd),
            pl.BlockSpec((1, 1, d), lambda i: (i // nb, 0, 5)),
            full(fng),
        ],
        out_specs=pl.BlockSpec((td, d), lambda i: (i, 0)),
        out_shape=jax.ShapeDtypeStruct((t, d), F32),
        scratch_shapes=[pltpu.VMEM((TOP_K, td, half), U32), pltpu.SemaphoreType.DMA(())],
        compiler_params=_params(("arbitrary",)),
        name="shared_combine",
    )(dest_t, x2, h2p, gate_rows, ys, wsg, wsu, wsd, mod, fng)


def kernel(x, c, w_ada, b_ada, norm1_g, norm2_g, w_in, b_forget, sc_conv_w, cf_conv_w, cf_conv_b,
           cf_ln_g, cf_ln_b, out_norm_g, w_out, w_router, router_bias, w_gate, w_up, w_down,
           ws_gate, ws_up, ws_down, final_norm_g):
    batch, seq, d = x.shape
    depth = w_ada.shape[0]
    t = batch * seq
    n_heads = b_forget.shape[1]
    attn_w = n_heads * HEAD_DIM
    sc_w = sc_conv_w.shape[2]
    cf_w = cf_conv_w.shape[2]
    conv_w = 3 * sc_w + 2 * cf_w
    assert w_in.shape[2] == 3 * attn_w + n_heads + conv_w
    assert batch <= V7X_SUBLANES and n_heads <= V7X_SUBLANES
    tl = _tiles(seq)
    n_assign = t * TOP_K
    n_blocks = -(-(n_assign + N_EXPERTS * (EXPERT_BLOCK - 1)) // EXPERT_BLOCK)
    n_rows = n_blocks * EXPERT_BLOCK

    c_pad = jnp.pad(c, ((0, V7X_SUBLANES - batch), (0, 0)))
    mod_all = _ada(c_pad, w_ada, b_ada)

    x2 = x.reshape(t, d)
    for l in range(depth):
        mod = mod_all[l, :batch].reshape(batch, 1, N_MOD * d)
        wl = w_in[l]
        w_t = jnp.concatenate([wl[:, :attn_w], wl[:, 2 * attn_w:3 * attn_w]], axis=1).T.astype(BF16)
        w_n = jnp.concatenate([wl[:, attn_w:2 * attn_w], wl[:, 3 * attn_w + n_heads:]], axis=1).astype(BF16)
        w_fcols = wl[:, 3 * attn_w:3 * attn_w + n_heads]
        w_f = jnp.pad(w_fcols, ((0, 0), (0, V7X_LANES - n_heads))).astype(BF16)
        w_ft = jnp.pad(w_fcols.T, ((0, 2 * V7X_SUBLANES - n_heads), (0, 0))).astype(BF16)
        qvt, k, conv_in, f_rows, f_t = _inproj(
            x2, norm1_g[l].reshape(1, d), mod, w_t, w_n, w_f, w_ft,
            seq=seq, attn_width=attn_w, conv_width=conv_w, tm=tl["tm_in"], tn=tl["tn_in"])

        bias_row = jnp.pad(b_forget[l], (0, V7X_LANES - n_heads)).reshape(1, V7X_LANES)
        bias_col = jnp.broadcast_to(
            jnp.pad(b_forget[l], (0, 2 * V7X_SUBLANES - n_heads))[:, None], (2 * V7X_SUBLANES, V7X_LANES))
        kf, qf = _fcum(f_rows, f_t, bias_row, bias_col, seq=seq, tc=tl["tc"], n_heads=n_heads)

        y_attn = _attention(qvt, qf, k, kf, batch=batch, seq=seq, n_heads=n_heads, tq=tl["tq"])

        scw = jnp.pad(sc_conv_w[l], ((0, V7X_SUBLANES - SC_KERNEL), (0, 0)))
        cfw = jnp.pad(cf_conv_w[l], ((0, CF_HALO - CF_KERNEL), (0, 0)))
        wr_t = w_router[l].T
        wr_hi = wr_t.astype(BF16)
        wr_lo = (wr_t - wr_hi.astype(F32)).astype(BF16)
        x2, h2p, logits_t = _mixer_out(
            x2, y_attn, conv_in, scw, cfw, cf_conv_b[l].reshape(1, cf_w), cf_ln_g[l].reshape(1, cf_w),
            cf_ln_b[l].reshape(1, cf_w), out_norm_g[l].reshape(1, d), w_out[l].astype(BF16), mod,
            norm2_g[l].reshape(1, d), wr_hi, wr_lo, seq=seq, tm=tl["tm_mix"])

        bias_b = jnp.broadcast_to(router_bias[l][:, None], (N_EXPERTS, tl["tr"]))
        eidx_t, gate_t, rank_t, counts = _router(logits_t, bias_b, tr=tl["tr"])
        counts = counts[:, 0]
        padded = (counts + EXPERT_BLOCK - 1) // EXPERT_BLOCK * EXPERT_BLOCK
        pad_end = jnp.cumsum(padded).astype(I32)
        offs = pad_end - padded
        dest_t = _dest(offs, eidx_t, rank_t, tr=tl["tr"])
        block_start = jnp.arange(n_blocks, dtype=I32) * EXPERT_BLOCK
        block_expert = jnp.minimum(
            jnp.sum((pad_end[None, :] <= block_start[:, None]).astype(I32), axis=1), N_EXPERTS - 1)
        used = (pad_end[-1:] // EXPERT_BLOCK).astype(I32)

        xs = _dispatch(pad_end, padded, dest_t, h2p, n_rows=n_rows, td=tl["td"])
        ys = _experts(block_expert, used, xs, w_gate, w_up, w_down, layer=l)
        x2 = _combine(dest_t, x2, h2p, gate_t.T, ys, ws_gate[l].astype(BF16), ws_up[l].astype(BF16),
                      ws_down[l].astype(BF16), mod, final_norm_g.reshape(1, d),
                      seq=seq, td=tl["td"], final_norm=(l == depth - 1))
    return x2.reshape(batch, seq, d)
```

```python
import functools
import math

import jax
import jax.numpy as jnp
from jax import lax
from jax.experimental import pallas as pl
from jax.experimental.pallas import tpu as pltpu

F32 = jnp.float32
BF16 = jnp.bfloat16
I32 = jnp.int32
U32 = jnp.uint32

V7X_LANES = 128
V7X_SUBLANES = 8
V7X_VMEM_BYTES = 64 * 1024 * 1024
VMEM_LIMIT = V7X_VMEM_BYTES - 8 * 1024 * 1024

HEAD_DIM = 128
N_EXPERTS = 64
TOP_K = 8
N_GROUPS = 8
GROUP_SIZE = N_EXPERTS // N_GROUPS
TOPK_GROUPS = 4
ROUTED_SCALE = 2.5
EXPERT_BLOCK = 256
SC_KERNEL = 3
CF_KERNEL = 31
N_MOD = 6
NORM_EPS = 1e-6
SC_HALO = 8
CF_HALO = 32
NEG_INF = float("-inf")
LOG2E = math.log2(math.e)
F_ROWS = 16
CONV_ROWS = 64
MIX_ROW_GROUPS = 2


def _params(semantics, **kw):
    return pltpu.CompilerParams(dimension_semantics=semantics, vmem_limit_bytes=VMEM_LIMIT, **kw)


def _tiles(seq):
    return dict(
        tm_in=min(1024, seq),
        tn_in=512,
        tc=min(512, seq),
        tq=min(512, seq),
        tm_mix=min(256, seq),
        tr=min(512, seq),
        td=min(256, seq),
    )


def _rms(x, g):
    return x * lax.rsqrt(jnp.mean(x * x, axis=-1, keepdims=True) + NORM_EPS) * g


def _silu(x):
    return x * jax.nn.sigmoid(x)


def _pack_bf16_pair(lo, hi):
    lo_b = lax.bitcast_convert_type(lo.astype(BF16).astype(F32), U32) >> 16
    hi_b = lax.bitcast_convert_type(hi.astype(BF16).astype(F32), U32) & jnp.uint32(0xFFFF0000)
    return hi_b | lo_b


def _unpack_bf16_pair(p):
    lo = lax.bitcast_convert_type(p << 16, F32)
    hi = lax.bitcast_convert_type(p & jnp.uint32(0xFFFF0000), F32)
    return lo, hi


def _split3(x):
    a = x.astype(BF16)
    r = x - a.astype(F32)
    b = r.astype(BF16)
    c = (r - b.astype(F32)).astype(BF16)
    return a, b, c


def _dot_nt(a, b):
    return lax.dot_general(a, b, (((1,), (1,)), ((), ())), preferred_element_type=F32)


def _ada_kernel(c_ref, w_ref, b_ref, o_ref):
    c = c_ref[...]
    ca = _silu(c).astype(BF16)
    o_ref[0] = jnp.dot(ca, w_ref[0].astype(BF16), preferred_element_type=F32) + b_ref[0]


def _ada(c_pad, w_ada, b_ada):
    depth, d, n = w_ada.shape
    tn = 1024
    return pl.pallas_call(
        _ada_kernel,
        grid=(depth, n // tn),
        in_specs=[
            pl.BlockSpec((V7X_SUBLANES, d), lambda l, j: (0, 0)),
            pl.BlockSpec((1, d, tn), lambda l, j: (l, 0, j)),
            pl.BlockSpec((1, 1, tn), lambda l, j: (l, 0, j)),
        ],
        out_specs=pl.BlockSpec((1, V7X_SUBLANES, tn), lambda l, j: (l, 0, j)),
        out_shape=jax.ShapeDtypeStruct((depth, V7X_SUBLANES, n), F32),
        compiler_params=_params(("arbitrary", "arbitrary")),
        name="ada_mod",
    )(c_pad, w_ada, b_ada.reshape(depth, 1, n))


def _inproj_kernel(x_ref, g_ref, sh_ref, sc_ref, wt_ref, w_ref, wf_ref, wft_ref,
                   qvt_ref, k_ref, conv_ref, fr_ref, ft_ref, h_scr,
                   *, n_q_steps, n_t_steps, n_k_steps, heads_per_step):
    j = pl.program_id(1)

    @pl.when(j == 0)
    def _():
        h = _rms(x_ref[...], g_ref[...]) * (1.0 + sc_ref[0]) + sh_ref[0]
        hb = h.astype(BF16)
        h_scr[...] = hb
        fr_ref[...] = jnp.dot(hb, wf_ref[...], preferred_element_type=F32)
        ft_ref[...] = _dot_nt(wft_ref[...], hb)

    @pl.when(j < n_t_steps)
    def _():
        scale = jnp.where(j < n_q_steps, LOG2E * HEAD_DIM ** -0.5, 1.0).astype(F32)
        a = _dot_nt(wt_ref[...], h_scr[...]) * scale
        for u in range(heads_per_step):
            qvt_ref[u] = a[u * HEAD_DIM:(u + 1) * HEAD_DIM, :].astype(BF16)

    @pl.when(j >= n_t_steps)
    def _():
        a = jnp.dot(h_scr[...], w_ref[...], preferred_element_type=F32)

        @pl.when(j < n_t_steps + n_k_steps)
        def _():
            for u in range(heads_per_step):
                k_ref[u] = a[:, u * HEAD_DIM:(u + 1) * HEAD_DIM].astype(BF16)

        @pl.when(j >= n_t_steps + n_k_steps)
        def _():
            conv_ref[...] = a.astype(BF16)


def _inproj(x2, g1, mod, w_t, w_n, w_f, w_ft, *, seq, attn_width, conv_width, tm, tn):
    t, d = x2.shape
    nb = seq // tm
    n_q_steps = attn_width // tn
    n_t_steps = 2 * attn_width // tn
    n_k_steps = attn_width // tn
    n_c_steps = conv_width // tn
    n_steps = n_t_steps + n_k_steps + n_c_steps
    hps = tn // HEAD_DIM
    n_heads = attn_width // HEAD_DIM
    kern = functools.partial(_inproj_kernel, n_q_steps=n_q_steps, n_t_steps=n_t_steps,
                             n_k_steps=n_k_steps, heads_per_step=hps)
    return pl.pallas_call(
        kern,
        grid=(t // tm, n_steps),
        in_specs=[
            pl.BlockSpec((tm, d), lambda i, j: (i, 0)),
            pl.BlockSpec((1, d), lambda i, j: (0, 0)),
            pl.BlockSpec((1, 1, d), lambda i, j: (i // nb, 0, 0)),
            pl.BlockSpec((1, 1, d), lambda i, j: (i // nb, 0, 1)),
            pl.BlockSpec((tn, d), lambda i, j: (jnp.minimum(j, n_t_steps - 1), 0)),
            pl.BlockSpec((d, tn), lambda i, j: (0, jnp.maximum(j - n_t_steps, 0))),
            pl.BlockSpec((d, V7X_LANES), lambda i, j: (0, 0)),
            pl.BlockSpec((2 * V7X_SUBLANES, d), lambda i, j: (0, 0)),
        ],
        out_specs=[
            pl.BlockSpec((hps, HEAD_DIM, tm), lambda i, j: (jnp.minimum(j, n_t_steps - 1), 0, i)),
            pl.BlockSpec((hps, tm, HEAD_DIM),
                         lambda i, j: (jnp.clip(j - n_t_steps, 0, n_k_steps - 1), i, 0)),
            pl.BlockSpec((tm, tn),
                         lambda i, j: (i, jnp.clip(j - n_t_steps - n_k_steps, 0, n_c_steps - 1))),
            pl.BlockSpec((tm, V7X_LANES), lambda i, j: (i, 0)),
            pl.BlockSpec((2 * V7X_SUBLANES, tm), lambda i, j: (0, i)),
        ],
        out_shape=[
            jax.ShapeDtypeStruct((2 * n_heads, HEAD_DIM, t), BF16),
            jax.ShapeDtypeStruct((n_heads, t, HEAD_DIM), BF16),
            jax.ShapeDtypeStruct((t, conv_width), BF16),
            jax.ShapeDtypeStruct((t, V7X_LANES), F32),
            jax.ShapeDtypeStruct((2 * V7X_SUBLANES, t), F32),
        ],
        scratch_shapes=[pltpu.VMEM((tm, d), BF16)],
        compiler_params=_params(("arbitrary", "arbitrary")),
        name="norm_inproj",
    )(x2, g1, mod, mod, w_t, w_n, w_f, w_ft)


def _log_sigmoid(z):
    return jnp.minimum(z, 0.0) - jnp.log1p(jnp.exp(-jnp.abs(z)))


def _fcum_kernel(fr_ref, ft_ref, br_ref, bt_ref, kf_ref, qf_ref, car_r, car_t, *, nb, tc, n_heads):
    i = pl.program_id(0)

    @pl.when(i % nb == 0)
    def _():
        car_r[...] = jnp.zeros_like(car_r)
        car_t[...] = jnp.zeros_like(car_t)

    row = lax.broadcasted_iota(I32, (tc, tc), 0)
    col = lax.broadcasted_iota(I32, (tc, tc), 1)
    lower = (row >= col).astype(BF16)
    upper = (row <= col).astype(BF16)

    lf_r = _log_sigmoid(fr_ref[...] + br_ref[...])
    cum_r = car_r[0:1, :]
    for part in _split3(lf_r):
        cum_r = cum_r + jnp.dot(lower, part, preferred_element_type=F32)
    car_r[...] = jnp.broadcast_to(cum_r[tc - 1:tc, :], car_r.shape)

    lf_t = _log_sigmoid(ft_ref[...] + bt_ref[:, 0:1])
    cum_t = car_t[:, 0:1]
    for part in _split3(lf_t):
        cum_t = cum_t + jnp.dot(part, upper, preferred_element_type=F32)
    car_t[...] = jnp.broadcast_to(cum_t[:, tc - 1:tc], car_t.shape)

    lane = lax.broadcasted_iota(I32, (tc, V7X_LANES), 1)
    sub = lax.broadcasted_iota(I32, (F_ROWS, tc), 0)
    for h in range(n_heads):
        k_hi, k_mid, k_lo = _split3(jnp.broadcast_to(cum_r[:, h:h + 1] * LOG2E, (tc, V7X_LANES)))
        ones = jnp.where(lane < 6, 1.0, 0.0).astype(BF16)
        kf_ref[h] = jnp.where(lane == 0, k_hi, jnp.where(lane == 1, k_mid, jnp.where(lane == 2, k_lo, ones)))
        q_hi, q_mid, q_lo = _split3(jnp.broadcast_to(cum_t[h:h + 1, :] * LOG2E, (F_ROWS, tc)))
        neg = jnp.where(sub < 3, -1.0, 0.0).astype(BF16)
        qf_ref[h] = jnp.where(sub == 3, q_hi, jnp.where(sub == 4, q_mid, jnp.where(sub == 5, q_lo, neg)))


def _fcum(f_rows, f_t, bias_row, bias_col, *, seq, tc, n_heads):
    t = f_rows.shape[0]
    nb = seq // tc
    rows_t = f_t.shape[0]
    return pl.pallas_call(
        functools.partial(_fcum_kernel, nb=nb, tc=tc, n_heads=n_heads),
        grid=(t // tc,),
        in_specs=[
            pl.BlockSpec((tc, V7X_LANES), lambda i: (i, 0)),
            pl.BlockSpec((rows_t, tc), lambda i: (0, i)),
            pl.BlockSpec((1, V7X_LANES), lambda i: (0, 0)),
            pl.BlockSpec((rows_t, V7X_LANES), lambda i: (0, 0)),
        ],
        out_specs=[
            pl.BlockSpec((n_heads, tc, V7X_LANES), lambda i: (0, i, 0)),
            pl.BlockSpec((n_heads, F_ROWS, tc), lambda i: (0, 0, i)),
        ],
        out_shape=[
            jax.ShapeDtypeStruct((n_heads, t, V7X_LANES), BF16),
            jax.ShapeDtypeStruct((n_heads, F_ROWS, t), BF16),
        ],
        scratch_shapes=[pltpu.VMEM((V7X_SUBLANES, V7X_LANES), F32),
                        pltpu.VMEM((rows_t, V7X_LANES), F32)],
        compiler_params=_params(("arbitrary",)),
        name="forget_cumsum",
    )(f_rows, f_t, bias_row, bias_col)


def _attn_kernel(q_ref, qf_ref, k_ref, kf_ref, vt_ref, o_ref, qa_scr, acc_scr, s0, s1, p0, p1, st_scr,
                 *, tq):
    i = pl.program_id(2)
    qa_scr[0:HEAD_DIM, :] = q_ref[0]
    qa_scr[HEAD_DIM:HEAD_DIM + F_ROWS, :] = qf_ref[0]
    qa_scr[HEAD_DIM + F_ROWS:, :] = jnp.zeros((HEAD_DIM - F_ROWS, tq), BF16)
    acc_scr[...] = jnp.zeros_like(acc_scr)
    p1[...] = jnp.zeros_like(p1)
    st_scr[0:1, :] = jnp.full((1, tq), NEG_INF, F32)
    st_scr[1:2, :] = jnp.zeros((1, tq), F32)
    st_scr[2:3, :] = jnp.ones((1, tq), F32)

    def logits(j, s_out):
        start = pl.multiple_of(j * tq, tq)
        kk = jnp.concatenate([k_ref[0, pl.ds(start, tq), :], kf_ref[0, pl.ds(start, tq), :]], axis=1)
        s_out[...] = jnp.dot(kk, qa_scr[...], preferred_element_type=F32)

    def values(j, p_in):
        start = pl.multiple_of(jnp.maximum(j, 0) * tq, tq)
        vt = vt_ref[0, :, pl.ds(start, tq)]
        acc_scr[...] = st_scr[2:3, :] * acc_scr[...] + jnp.dot(vt, p_in[...], preferred_element_type=F32)

    def softmax(s_in, p_out, masked):
        s = s_in[...]
        if masked:
            key = lax.broadcasted_iota(I32, (tq, tq), 0)
            qry = lax.broadcasted_iota(I32, (tq, tq), 1)
            s = jnp.where(key <= qry, s, NEG_INF)
        m_prev = st_scr[0:1, :]
        m_new = jnp.maximum(m_prev, jnp.max(s, axis=0, keepdims=True))
        alpha = jnp.exp2(m_prev - m_new)
        p = jnp.exp2(s - m_new)
        st_scr[1:2, :] = alpha * st_scr[1:2, :] + jnp.sum(p, axis=0, keepdims=True)
        st_scr[0:1, :] = m_new
        st_scr[2:3, :] = alpha
        p_out[...] = p.astype(BF16)

    def stage(j, s_cur, s_nxt, p_cur, p_prev):
        logits(j + 1, s_nxt)
        values(j - 1, p_prev)
        softmax(s_cur, p_cur, False)

    def tail(s_cur, p_cur, p_prev):
        values(i - 1, p_prev)
        softmax(s_cur, p_cur, True)
        values(i, p_cur)
        o_ref[...] = (acc_scr[...] / st_scr[1:2, :]).T

    logits(0, s0)

    def pair(t, carry):
        stage(2 * t, s0, s1, p0, p1)
        stage(2 * t + 1, s1, s0, p1, p0)
        return carry

    lax.fori_loop(0, i // 2, pair, 0)

    @pl.when(i % 2 == 0)
    def _():
        tail(s0, p0, p1)

    @pl.when(i % 2 == 1)
    def _():
        stage(i - 1, s0, s1, p0, p1)
        tail(s1, p1, p0)


def _attention(qvt, qf, k, kf, *, batch, seq, n_heads, tq):
    t = batch * seq
    nq = seq // tq
    return pl.pallas_call(
        functools.partial(_attn_kernel, tq=tq),
        grid=(batch, n_heads, nq),
        in_specs=[
            pl.BlockSpec((1, HEAD_DIM, tq), lambda b, h, i: (h, 0, b * nq + i)),
            pl.BlockSpec((1, F_ROWS, tq), lambda b, h, i: (h, 0, b * nq + i)),
            pl.BlockSpec((1, seq, HEAD_DIM), lambda b, h, i: (h, b, 0)),
            pl.BlockSpec((1, seq, V7X_LANES), lambda b, h, i: (h, b, 0)),
            pl.BlockSpec((1, HEAD_DIM, seq), lambda b, h, i: (n_heads + h, 0, b)),
        ],
        out_specs=pl.BlockSpec((tq, HEAD_DIM), lambda b, h, i: (b * nq + i, h)),
        out_shape=jax.ShapeDtypeStruct((t, n_heads * HEAD_DIM), F32),
        scratch_shapes=[pltpu.VMEM((2 * HEAD_DIM, tq), BF16), pltpu.VMEM((HEAD_DIM, tq), F32),
                        pltpu.VMEM((tq, tq), F32), pltpu.VMEM((tq, tq), F32),
                        pltpu.VMEM((tq, tq), BF16), pltpu.VMEM((tq, tq), BF16),
                        pltpu.VMEM((V7X_SUBLANES, tq), F32)],
        compiler_params=_params(("arbitrary", "arbitrary", "arbitrary")),
        name="forget_attention",
    )(qvt, qf, k, kf, qvt)


def _mixer_out_kernel(x_ref, ya_ref, cv_ref, scw_ref, cfw_ref, cfb_ref, lng_ref, lnb_ref, ong_ref,
                      wout_ref, g1_ref, n2g_ref, sh2_ref, sc2_ref, wr_hi_ref, wr_lo_ref,
                      xo_ref, hp_ref, lg_ref, zbuf, ubuf, conv_scr, *, nb, tm, sc_w, cf_w, attn_w):
    i = pl.program_id(0)

    @pl.when(i % nb == 0)
    def _():
        zbuf[0:SC_HALO, :] = jnp.zeros((SC_HALO, sc_w), F32)
        ubuf[0:CF_HALO, :] = jnp.zeros((CF_HALO, cf_w), F32)
        ubuf[CF_HALO + tm:, :] = jnp.zeros((V7X_SUBLANES, cf_w), F32)

    cv = cv_ref[...].astype(F32)
    sc_b = cv[:, 0:sc_w]
    sc_c = cv[:, sc_w:2 * sc_w]
    sc_u = cv[:, 2 * sc_w:3 * sc_w]
    cf_a = cv[:, 3 * sc_w:3 * sc_w + cf_w]
    cf_gate = cv[:, 3 * sc_w + cf_w:3 * sc_w + 2 * cf_w]

    zbuf[SC_HALO:SC_HALO + tm, :] = sc_c * sc_u
    conv = jnp.zeros((tm, sc_w), F32)
    for k in range(SC_KERNEL):
        off = SC_HALO - (SC_KERNEL - 1) + k
        conv = conv + scw_ref[k:k + 1, :] * zbuf[off:off + tm, :]
    y_sc = sc_b * conv
    zbuf[0:SC_HALO, :] = zbuf[tm:tm + SC_HALO, :]

    ubuf[CF_HALO:CF_HALO + tm, :] = cf_a * jax.nn.sigmoid(cf_gate)
    def conv_chunk(c, carry):
        r0 = pl.multiple_of((c // (cf_w // V7X_LANES)) * CONV_ROWS, CONV_ROWS)
        c0 = pl.multiple_of((c % (cf_w // V7X_LANES)) * V7X_LANES, V7X_LANES)
        cols = pl.ds(c0, V7X_LANES)
        acc = jnp.zeros((CONV_ROWS, V7X_LANES), F32) + cfb_ref[:, cols]
        for phase in range(V7X_SUBLANES):
            z = None
            for k in range(CF_KERNEL):
                off = CF_HALO - (CF_KERNEL - 1) + k
                if off % V7X_SUBLANES != phase:
                    continue
                win = pl.ds(pl.multiple_of(r0 + (off - phase), V7X_SUBLANES), CONV_ROWS + V7X_SUBLANES)
                term = cfw_ref[k:k + 1, cols] * ubuf[win, cols]
                z = term if z is None else z + term
            if z is not None:
                acc = acc + z[phase:phase + CONV_ROWS, :]
        conv_scr[pl.ds(r0, CONV_ROWS), cols] = acc
        return carry

    lax.fori_loop(0, (tm // CONV_ROWS) * (cf_w // V7X_LANES), conv_chunk, 0)
    ubuf[0:CF_HALO, :] = ubuf[tm:tm + CF_HALO, :]
    ong = ong_ref[...]
    w_hi = wr_hi_ref[...]
    rg = tm // MIX_ROW_GROUPS
    for grp in range(MIX_ROW_GROUPS):
        rows = slice(grp * rg, (grp + 1) * rg)
        conv = conv_scr[rows, :]
        mu = jnp.mean(conv, axis=-1, keepdims=True)
        cen = conv - mu
        var = jnp.mean(cen * cen, axis=-1, keepdims=True)
        y_cf = _silu(cen * lax.rsqrt(var + NORM_EPS) * lng_ref[...] + lnb_ref[...])
        y = jnp.concatenate([
            _rms(ya_ref[rows, :], ong[:, 0:attn_w]),
            _rms(y_sc[rows, :], ong[:, attn_w:attn_w + sc_w]),
            _rms(y_cf, ong[:, attn_w + sc_w:attn_w + sc_w + cf_w]),
        ], axis=-1).astype(BF16)
        mix = jnp.dot(y, wout_ref[...], preferred_element_type=F32)
        x_new = x_ref[rows, :] + g1_ref[0] * mix
        xo_ref[rows, :] = x_new

        h2 = _rms(x_new, n2g_ref[...]) * (1.0 + sc2_ref[0]) + sh2_ref[0]
        half = h2.shape[-1] // 2
        hp_ref[rows, :] = _pack_bf16_pair(h2[:, :half], h2[:, half:])
        h_hi = h2.astype(BF16)
        h_lo = (h2 - h_hi.astype(F32)).astype(BF16)
        lg_ref[:, rows] = _dot_nt(w_hi, h_hi) + _dot_nt(w_hi, h_lo) + _dot_nt(wr_lo_ref[...], h_hi)


def _mixer_out(x2, y_attn, conv_in, scw, cfw, cfb, lng, lnb, ong, w_out, mod, n2g, wr_hi, wr_lo,
               *, seq, tm):
    t, d = x2.shape
    attn_w = y_attn.shape[1]
    sc_w = scw.shape[1]
    cf_w = cfw.shape[1]
    conv_w = conv_in.shape[1]
    nb = seq // tm
    n_exp = wr_hi.shape[0]
    full = lambda a: pl.BlockSpec(a.shape, lambda i: (0,) * a.ndim)
    modspec = lambda c: pl.BlockSpec((1, 1, d), lambda i: (i // nb, 0, c))
    kern = functools.partial(_mixer_out_kernel, nb=nb, tm=tm, sc_w=sc_w, cf_w=cf_w, attn_w=attn_w)
    return pl.pallas_call(
        kern,
        grid=(t // tm,),
        in_specs=[
            pl.BlockSpec((tm, d), lambda i: (i, 0)),
            pl.BlockSpec((tm, attn_w), lambda i: (i, 0)),
            pl.BlockSpec((tm, conv_w), lambda i: (i, 0)),
            full(scw), full(cfw), full(cfb), full(lng), full(lnb), full(ong), full(w_out),
            modspec(2), full(n2g), modspec(3), modspec(4), full(wr_hi), full(wr_lo),
        ],
        out_specs=[
            pl.BlockSpec((tm, d), lambda i: (i, 0)),
            pl.BlockSpec((tm, d // 2), lambda i: (i, 0)),
            pl.BlockSpec((n_exp, tm), lambda i: (0, i)),
        ],
        out_shape=[
            jax.ShapeDtypeStruct((t, d), F32),
            jax.ShapeDtypeStruct((t, d // 2), U32),
            jax.ShapeDtypeStruct((n_exp, t), F32),
        ],
        scratch_shapes=[pltpu.VMEM((SC_HALO + tm, sc_w), F32),
                        pltpu.VMEM((CF_HALO + tm + V7X_SUBLANES, cf_w), F32),
                        pltpu.VMEM((tm, cf_w), F32)],
        compiler_params=_params(("arbitrary",)),
        name="conv_norm_outproj",
    )(x2, y_attn, conv_in, scw, cfw, cfb, lng, lnb, ong, w_out, mod, n2g, mod, mod, wr_hi, wr_lo)


def _col_max(x):
    return jnp.max(x, axis=0, keepdims=True)


def _col_min(x):
    return jnp.min(x, axis=0, keepdims=True)


def _router_kernel(lg_ref, bias_ref, eidx_ref, gate_ref, rank_ref, cnt_ref, carry, *, tr):
    @pl.when(pl.program_id(0) == 0)
    def _():
        carry[...] = jnp.zeros_like(carry)

    scores = jax.nn.sigmoid(lg_ref[...])
    biased = scores + bias_ref[...]
    rid = lax.broadcasted_iota(I32, (GROUP_SIZE, tr), 0)
    sc_g = [scores[GROUP_SIZE * g:GROUP_SIZE * (g + 1)] for g in range(N_GROUPS)]
    bi_g = [biased[GROUP_SIZE * g:GROUP_SIZE * (g + 1)] for g in range(N_GROUPS)]

    gs = []
    for g in range(N_GROUPS):
        m1 = _col_max(bi_g[g])
        first = _col_min(jnp.where(bi_g[g] == m1, rid, GROUP_SIZE))
        m2 = _col_max(jnp.where(rid == first, NEG_INF, bi_g[g]))
        gs.append(m1 + m2)
    grp = jnp.concatenate(gs, axis=0)

    keep = jnp.zeros((N_GROUPS, tr), F32)
    for _ in range(TOPK_GROUPS):
        m = _col_max(grp)
        first = _col_min(jnp.where(grp == m, rid, N_GROUPS))
        hit = rid == first
        keep = jnp.where(hit, 1.0, keep)
        grp = jnp.where(hit, NEG_INF, grp)

    cand = [jnp.where(keep[g:g + 1, :] > 0.5, bi_g[g], NEG_INF) for g in range(N_GROUPS)]
    eid_g = [rid + GROUP_SIZE * g for g in range(N_GROUPS)]
    onehot = [jnp.zeros((GROUP_SIZE, tr), F32) for _ in range(N_GROUPS)]
    e_sel, g_sel = [], []
    for _ in range(TOP_K):
        m = cand[0]
        for g in range(1, N_GROUPS):
            m = jnp.maximum(m, cand[g])
        m = _col_max(m)
        first = jnp.where(cand[0] == m, eid_g[0], N_EXPERTS)
        for g in range(1, N_GROUPS):
            first = jnp.minimum(first, jnp.where(cand[g] == m, eid_g[g], N_EXPERTS))
        first = _col_min(first)
        gate = jnp.zeros((GROUP_SIZE, tr), F32)
        for g in range(N_GROUPS):
            hit = eid_g[g] == first
            gate = gate + jnp.where(hit, sc_g[g], 0.0)
            cand[g] = jnp.where(hit, NEG_INF, cand[g])
            onehot[g] = jnp.where(hit, 1.0, onehot[g])
        e_sel.append(first)
        g_sel.append(jnp.sum(gate, axis=0, keepdims=True))

    gates = jnp.concatenate(g_sel, axis=0)
    gates = gates / (jnp.sum(gates, axis=0, keepdims=True) + 1e-20) * ROUTED_SCALE
    eidx = jnp.concatenate(e_sel, axis=0)
    eidx_ref[...] = eidx
    gate_ref[...] = gates

    oh = jnp.concatenate(onehot, axis=0)
    srow = lax.broadcasted_iota(I32, (tr, tr), 0)
    scol = lax.broadcasted_iota(I32, (tr, tr), 1)
    strict = (srow < scol).astype(BF16)
    before = jnp.dot(oh.astype(BF16), strict, preferred_element_type=F32) + carry[:, 0:1]
    ranks = []
    for k in range(TOP_K):
        acc = jnp.zeros((GROUP_SIZE, tr), F32)
        for g in range(N_GROUPS):
            acc = acc + jnp.where(eid_g[g] == e_sel[k], before[GROUP_SIZE * g:GROUP_SIZE * (g + 1)], 0.0)
        ranks.append(jnp.sum(acc, axis=0, keepdims=True))
    rank_ref[...] = jnp.concatenate(ranks, axis=0).astype(I32)
    carry[...] = carry[...] + jnp.sum(oh, axis=1, keepdims=True)
    cnt_ref[...] = carry[...].astype(I32)


def _router(logits_t, bias_b, *, tr):
    n_exp, t = logits_t.shape
    return pl.pallas_call(
        functools.partial(_router_kernel, tr=tr),
        grid=(t // tr,),
        in_specs=[
            pl.BlockSpec((n_exp, tr), lambda i: (0, i)),
            pl.BlockSpec((n_exp, tr), lambda i: (0, 0)),
        ],
        out_specs=[
            pl.BlockSpec((TOP_K, tr), lambda i: (0, i)),
            pl.BlockSpec((TOP_K, tr), lambda i: (0, i)),
            pl.BlockSpec((TOP_K, tr), lambda i: (0, i)),
            pl.BlockSpec((n_exp, V7X_LANES), lambda i: (0, 0)),
        ],
        out_shape=[
            jax.ShapeDtypeStruct((TOP_K, t), I32),
            jax.ShapeDtypeStruct((TOP_K, t), F32),
            jax.ShapeDtypeStruct((TOP_K, t), I32),
            jax.ShapeDtypeStruct((n_exp, V7X_LANES), I32),
        ],
        scratch_shapes=[pltpu.VMEM((n_exp, V7X_LANES), F32)],
        compiler_params=_params(("arbitrary",)),
        name="router_select",
    )(logits_t, bias_b)


def _dest_kernel(offs_ref, eidx_ref, rank_ref, dest_ref):
    eidx = eidx_ref[...]
    base = jnp.zeros(eidx.shape, I32)
    for e in range(N_EXPERTS):
        base = jnp.where(eidx == e, offs_ref[e], base)
    dest_ref[...] = base + rank_ref[...]


def _dest(offs, eidx_t, rank_t, *, tr):
    t = eidx_t.shape[1]
    return pl.pallas_call(
        _dest_kernel,
        grid_spec=pltpu.PrefetchScalarGridSpec(
            num_scalar_prefetch=1,
            grid=(t // tr,),
            in_specs=[pl.BlockSpec((TOP_K, tr), lambda i, o: (0, i)),
                      pl.BlockSpec((TOP_K, tr), lambda i, o: (0, i))],
            out_specs=pl.BlockSpec((TOP_K, tr), lambda i, o: (0, i)),
        ),
        out_shape=jax.ShapeDtypeStruct((TOP_K, t), I32),
        compiler_params=_params(("arbitrary",)),
        name="dispatch_rows",
    )(offs, eidx_t, rank_t)


def _dispatch_kernel(pad_end_ref, padded_ref, dest_ref, hp_ref, xs_ref, zero_scr, sem, *, td):
    i = pl.program_id(0)

    def row_copy(t, k):
        return pltpu.make_async_copy(hp_ref.at[pl.ds(t, 1)], xs_ref.at[pl.ds(dest_ref[k, t], 1)], sem)

    @pl.when(i == 0)
    def _():
        zero_scr[...] = jnp.zeros_like(zero_scr)

        def zero_copy(e):
            start = pl.multiple_of(pad_end_ref[e] - EXPERT_BLOCK, EXPERT_BLOCK)
            return pltpu.make_async_copy(zero_scr, xs_ref.at[pl.ds(start, EXPERT_BLOCK)], sem)

        def zstart(e, c):
            @pl.when(padded_ref[e] > 0)
            def _():
                zero_copy(e).start()
            return c

        def zwait(e, c):
            @pl.when(padded_ref[e] > 0)
            def _():
                zero_copy(e).wait()
            return c

        lax.fori_loop(0, N_EXPERTS, zstart, 0)
        lax.fori_loop(0, N_EXPERTS, zwait, 0)

    def start(t, c):
        for k in range(TOP_K):
            row_copy(t, k).start()
        return c

    def wait(t, c):
        for k in range(TOP_K):
            row_copy(t, k).wait()
        return c

    lax.fori_loop(0, td, start, 0)
    lax.fori_loop(0, td, wait, 0)


def _dispatch(pad_end, padded, dest_t, h2p, *, n_rows, td):
    t, half = h2p.shape
    return pl.pallas_call(
        functools.partial(_dispatch_kernel, td=td),
        grid_spec=pltpu.PrefetchScalarGridSpec(
            num_scalar_prefetch=2,
            grid=(t // td,),
            in_specs=[
                pl.BlockSpec((TOP_K, td), lambda i, a, b: (0, i), memory_space=pltpu.SMEM),
                pl.BlockSpec((td, half), lambda i, a, b: (i, 0)),
            ],
            out_specs=pl.BlockSpec(memory_space=pl.ANY),
            scratch_shapes=[pltpu.VMEM((EXPERT_BLOCK, half), U32), pltpu.SemaphoreType.DMA(())],
        ),
        out_shape=jax.ShapeDtypeStruct((n_rows, half), U32),
        compiler_params=_params(("arbitrary",)),
        name="dispatch_scatter",
    )(pad_end, padded, dest_t, h2p)


def _expert_kernel(be_ref, used_ref, xs_ref, wg_ref, wu_ref, wd_ref, ys_ref, wg_s, wu_s, wd_s):
    n = pl.program_id(0)

    @pl.when((n < used_ref[0]) & ((n == 0) | (be_ref[n] != be_ref[jnp.maximum(n - 1, 0)])))
    def _():
        wg_s[...] = wg_ref[0].astype(BF16)
        wu_s[...] = wu_ref[0].astype(BF16)
        wd_s[...] = wd_ref[0].astype(BF16)

    @pl.when(n < used_ref[0])
    def _():
        lo, hi = _unpack_bf16_pair(xs_ref[...])
        lo = lo.astype(BF16)
        hi = hi.astype(BF16)
        half = lo.shape[-1]
        g = (jnp.dot(lo, wg_s[:half, :], preferred_element_type=F32)
             + jnp.dot(hi, wg_s[half:, :], preferred_element_type=F32))
        u = (jnp.dot(lo, wu_s[:half, :], preferred_element_type=F32)
             + jnp.dot(hi, wu_s[half:, :], preferred_element_type=F32))
        hid = (_silu(g) * u).astype(BF16)
        y = jnp.dot(hid, wd_s[...], preferred_element_type=F32)
        ys_ref[...] = _pack_bf16_pair(y[:, :half], y[:, half:])


def _experts(block_expert, used, xs, wg, wu, wd, *, layer):
    n_rows, half = xs.shape
    n_blocks = n_rows // EXPERT_BLOCK
    _, _, d, hid = wg.shape
    blk = lambda n, be, used: (jnp.minimum(n, used[0] - 1), 0)
    wsel = lambda n, be, used: (layer, be[jnp.minimum(n, used[0] - 1)], 0, 0)
    return pl.pallas_call(
        _expert_kernel,
        grid_spec=pltpu.PrefetchScalarGridSpec(
            num_scalar_prefetch=2,
            grid=(n_blocks,),
            in_specs=[
                pl.BlockSpec((EXPERT_BLOCK, half), blk),
                pl.BlockSpec((None, 1, d, hid), wsel),
                pl.BlockSpec((None, 1, d, hid), wsel),
                pl.BlockSpec((None, 1, hid, d), wsel),
            ],
            out_specs=pl.BlockSpec((EXPERT_BLOCK, half), blk),
            scratch_shapes=[pltpu.VMEM((d, hid), BF16), pltpu.VMEM((d, hid), BF16),
                            pltpu.VMEM((hid, d), BF16)],
        ),
        out_shape=jax.ShapeDtypeStruct((n_rows, half), U32),
        compiler_params=_params(("arbitrary",)),
        name="expert_ffn",
    )(block_expert, used, xs, wg, wu, wd)


def _combine_kernel(dest_ref, x_ref, hp_ref, gate_ref, ys_ref, wsg_ref, wsu_ref, wsd_ref, g2_ref, fng_ref,
                    o_ref, gbuf, sem, *, td, final_norm):
    def row_copy(t, k):
        return pltpu.make_async_copy(ys_ref.at[pl.ds(dest_ref[k, t], 1)], gbuf.at[k, pl.ds(t, 1)], sem)

    def start(t, c):
        for k in range(TOP_K):
            row_copy(t, k).start()
        return c

    def wait(t, c):
        for k in range(TOP_K):
            row_copy(t, k).wait()
        return c

    lax.fori_loop(0, td, start, 0)

    lo, hi = _unpack_bf16_pair(hp_ref[...])
    lo = lo.astype(BF16)
    hi = hi.astype(BF16)
    half = lo.shape[-1]
    g = (jnp.dot(lo, wsg_ref[:half, :], preferred_element_type=F32)
         + jnp.dot(hi, wsg_ref[half:, :], preferred_element_type=F32))
    u = (jnp.dot(lo, wsu_ref[:half, :], preferred_element_type=F32)
         + jnp.dot(hi, wsu_ref[half:, :], preferred_element_type=F32))
    hid = (_silu(g) * u).astype(BF16)
    shared = jnp.dot(hid, wsd_ref[...], preferred_element_type=F32)

    lax.fori_loop(0, td, wait, 0)

    gate = gate_ref[...]
    r_lo = jnp.zeros((td, half), F32)
    r_hi = jnp.zeros((td, half), F32)
    for k in range(TOP_K):
        y_lo, y_hi = _unpack_bf16_pair(gbuf[k])
        w = gate[:, k:k + 1]
        r_lo = r_lo + w * y_lo
        r_hi = r_hi + w * y_hi
    routed = jnp.concatenate([r_lo, r_hi], axis=-1)
    out = x_ref[...] + g2_ref[0] * (routed + shared)
    if final_norm:
        out = _rms(out, fng_ref[...])
    o_ref[...] = out


def _combine(dest_t, x2, h2p, gate_rows, ys, wsg, wsu, wsd, mod, fng, *, seq, td, final_norm):
    t, d = x2.shape
    half = d // 2
    nb = seq // td
    full = lambda a: pl.BlockSpec(a.shape, lambda i: (0,) * a.ndim)
    return pl.pallas_call(
        functools.partial(_combine_kernel, td=td, final_norm=final_norm),
        grid=(t // td,),
        in_specs=[
            pl.BlockSpec((TOP_K, td), lambda i: (0, i), memory_space=pltpu.SMEM),
            pl.BlockSpec((td, d), lambda i: (i, 0)),
            pl.BlockSpec((td, half), lambda i: (i, 0)),
            pl.BlockSpec((td, TOP_K), lambda i: (i, 0)),
            pl.BlockSpec(memory_space=pl.ANY),
            full(wsg), full(wsu), full(wsd),
            pl.BlockSpec((1, 1, d), lambda i: (i // nb, 0, 5)),
            full(fng),
        ],
        out_specs=pl.BlockSpec((td, d), lambda i: (i, 0)),
        out_shape=jax.ShapeDtypeStruct((t, d), F32),
        scratch_shapes=[pltpu.VMEM((TOP_K, td, half), U32), pltpu.SemaphoreType.DMA(())],
        compiler_params=_params(("arbitrary",)),
        name="shared_combine",
    )(dest_t, x2, h2p, gate_rows, ys, wsg, wsu, wsd, mod, fng)


def kernel(x, c, w_ada, b_ada, norm1_g, norm2_g, w_in, b_forget, sc_conv_w, cf_conv_w, cf_conv_b,
           cf_ln_g, cf_ln_b, out_norm_g, w_out, w_router, router_bias, w_gate, w_up, w_down,
           ws_gate, ws_up, ws_down, final_norm_g):
    batch, seq, d = x.shape
    depth = w_ada.shape[0]
    t = batch * seq
    n_heads = b_forget.shape[1]
    attn_w = n_heads * HEAD_DIM
    sc_w = sc_conv_w.shape[2]
    cf_w = cf_conv_w.shape[2]
    conv_w = 3 * sc_w + 2 * cf_w
    assert w_in.shape[2] == 3 * attn_w + n_heads + conv_w
    assert batch <= V7X_SUBLANES and n_heads <= V7X_SUBLANES
    tl = _tiles(seq)
    n_assign = t * TOP_K
    n_blocks = -(-(n_assign + N_EXPERTS * (EXPERT_BLOCK - 1)) // EXPERT_BLOCK)
    n_rows = n_blocks * EXPERT_BLOCK

    c_pad = jnp.pad(c, ((0, V7X_SUBLANES - batch), (0, 0)))
    mod_all = _ada(c_pad, w_ada, b_ada)

    x2 = x.reshape(t, d)
    for l in range(depth):
        mod = mod_all[l, :batch].reshape(batch, 1, N_MOD * d)
        wl = w_in[l]
        w_t = jnp.concatenate([wl[:, :attn_w], wl[:, 2 * attn_w:3 * attn_w]], axis=1).T.astype(BF16)
        w_n = jnp.concatenate([wl[:, attn_w:2 * attn_w], wl[:, 3 * attn_w + n_heads:]], axis=1).astype(BF16)
        w_fcols = wl[:, 3 * attn_w:3 * attn_w + n_heads]
        w_f = jnp.pad(w_fcols, ((0, 0), (0, V7X_LANES - n_heads))).astype(BF16)
        w_ft = jnp.pad(w_fcols.T, ((0, 2 * V7X_SUBLANES - n_heads), (0, 0))).astype(BF16)
        qvt, k, conv_in, f_rows, f_t = _inproj(
            x2, norm1_g[l].reshape(1, d), mod, w_t, w_n, w_f, w_ft,
            seq=seq, attn_width=attn_w, conv_width=conv_w, tm=tl["tm_in"], tn=tl["tn_in"])

        bias_row = jnp.pad(b_forget[l], (0, V7X_LANES - n_heads)).reshape(1, V7X_LANES)
        bias_col = jnp.broadcast_to(
            jnp.pad(b_forget[l], (0, 2 * V7X_SUBLANES - n_heads))[:, None], (2 * V7X_SUBLANES, V7X_LANES))
        kf, qf = _fcum(f_rows, f_t, bias_row, bias_col, seq=seq, tc=tl["tc"], n_heads=n_heads)

        y_attn = _attention(qvt, qf, k, kf, batch=batch, seq=seq, n_heads=n_heads, tq=tl["tq"])

        scw = jnp.pad(sc_conv_w[l], ((0, V7X_SUBLANES - SC_KERNEL), (0, 0)))
        cfw = jnp.pad(cf_conv_w[l], ((0, CF_HALO - CF_KERNEL), (0, 0)))
        wr_t = w_router[l].T
        wr_hi = wr_t.astype(BF16)
        wr_lo = (wr_t - wr_hi.astype(F32)).astype(BF16)
        x2, h2p, logits_t = _mixer_out(
            x2, y_attn, conv_in, scw, cfw, cf_conv_b[l].reshape(1, cf_w), cf_ln_g[l].reshape(1, cf_w),
            cf_ln_b[l].reshape(1, cf_w), out_norm_g[l].reshape(1, d), w_out[l].astype(BF16), mod,
            norm2_g[l].reshape(1, d), wr_hi, wr_lo, seq=seq, tm=tl["tm_mix"])

        bias_b = jnp.broadcast_to(router_bias[l][:, None], (N_EXPERTS, tl["tr"]))
        eidx_t, gate_t, rank_t, counts = _router(logits_t, bias_b, tr=tl["tr"])
        counts = counts[:, 0]
        padded = (counts + EXPERT_BLOCK - 1) // EXPERT_BLOCK * EXPERT_BLOCK
        pad_end = jnp.cumsum(padded).astype(I32)
        offs = pad_end - padded
        dest_t = _dest(offs, eidx_t, rank_t, tr=tl["tr"])
        block_start = jnp.arange(n_blocks, dtype=I32) * EXPERT_BLOCK
        block_expert = jnp.minimum(
            jnp.sum((pad_end[None, :] <= block_start[:, None]).astype(I32), axis=1), N_EXPERTS - 1)
        used = (pad_end[-1:] // EXPERT_BLOCK).astype(I32)

        xs = _dispatch(pad_end, padded, dest_t, h2p, n_rows=n_rows, td=tl["td"])
        ys = _experts(block_expert, used, xs, w_gate, w_up, w_down, layer=l)
        x2 = _combine(dest_t, x2, h2p, gate_t.T, ys, ws_gate[l].astype(BF16), ws_up[l].astype(BF16),
                      ws_down[l].astype(BF16), mod, final_norm_g.reshape(1, d),
                      seq=seq, td=tl["td"], final_norm=(l == depth - 1))
    return x2.reshape(batch, seq, d)
```

```python
import functools
import math

import jax
import jax.numpy as jnp
from jax import lax
from jax.experimental import pallas as pl
from jax.experimental.pallas import tpu as pltpu

F32 = jnp.float32
BF16 = jnp.bfloat16
I32 = jnp.int32
U32 = jnp.uint32

V7X_LANES = 128
V7X_SUBLANES = 8
V7X_VMEM_BYTES = 64 * 1024 * 1024
VMEM_LIMIT = V7X_VMEM_BYTES - 8 * 1024 * 1024

HEAD_DIM = 128
N_EXPERTS = 64
TOP_K = 8
N_GROUPS = 8
GROUP_SIZE = N_EXPERTS // N_GROUPS
TOPK_GROUPS = 4
ROUTED_SCALE = 2.5
EXPERT_BLOCK = 256
SC_KERNEL = 3
CF_KERNEL = 31
N_MOD = 6
NORM_EPS = 1e-6
SC_HALO = 8
CF_HALO = 32
NEG_INF = float("-inf")
LOG2E = math.log2(math.e)
F_ROWS = 16
CONV_ROWS = 64
MIX_ROW_GROUPS = 2


def _params(semantics, **kw):
    return pltpu.CompilerParams(dimension_semantics=semantics, vmem_limit_bytes=VMEM_LIMIT, **kw)


def _tiles(seq):
    return dict(
        tm_in=min(1024, seq),
        tn_in=512,
        tc=min(512, seq),
        tq=min(512, seq),
        tm_mix=min(256, seq),
        tr=min(512, seq),
        td=min(256, seq),
    )


def _rms(x, g):
    return x * lax.rsqrt(jnp.mean(x * x, axis=-1, keepdims=True) + NORM_EPS) * g


def _silu(x):
    return x * jax.nn.sigmoid(x)


def _pack_bf16_pair(lo, hi):
    lo_b = lax.bitcast_convert_type(lo.astype(BF16).astype(F32), U32) >> 16
    hi_b = lax.bitcast_convert_type(hi.astype(BF16).astype(F32), U32) & jnp.uint32(0xFFFF0000)
    return hi_b | lo_b


def _unpack_bf16_pair(p):
    lo = lax.bitcast_convert_type(p << 16, F32)
    hi = lax.bitcast_convert_type(p & jnp.uint32(0xFFFF0000), F32)
    return lo, hi


def _split3(x):
    a = x.astype(BF16)
    r = x - a.astype(F32)
    b = r.astype(BF16)
    c = (r - b.astype(F32)).astype(BF16)
    return a, b, c


def _dot_nt(a, b):
    return lax.dot_general(a, b, (((1,), (1,)), ((), ())), preferred_element_type=F32)


def _ada_kernel(c_ref, w_ref, b_ref, o_ref):
    c = c_ref[...]
    ca = _silu(c).astype(BF16)
    o_ref[0] = jnp.dot(ca, w_ref[0].astype(BF16), preferred_element_type=F32) + b_ref[0]


def _ada(c_pad, w_ada, b_ada):
    depth, d, n = w_ada.shape
    tn = 1024
    return pl.pallas_call(
        _ada_kernel,
        grid=(depth, n // tn),
        in_specs=[
            pl.BlockSpec((V7X_SUBLANES, d), lambda l, j: (0, 0)),
            pl.BlockSpec((1, d, tn), lambda l, j: (l, 0, j)),
            pl.BlockSpec((1, 1, tn), lambda l, j: (l, 0, j)),
        ],
        out_specs=pl.BlockSpec((1, V7X_SUBLANES, tn), lambda l, j: (l, 0, j)),
        out_shape=jax.ShapeDtypeStruct((depth, V7X_SUBLANES, n), F32),
        compiler_params=_params(("arbitrary", "arbitrary")),
        name="ada_mod",
    )(c_pad, w_ada, b_ada.reshape(depth, 1, n))


def _inproj_kernel(x_ref, g_ref, sh_ref, sc_ref, wt_ref, w_ref, wf_ref, wft_ref,
                   qvt_ref, k_ref, conv_ref, fr_ref, ft_ref, h_scr,
                   *, n_q_steps, n_t_steps, n_k_steps, heads_per_step):
    j = pl.program_id(1)

    @pl.when(j == 0)
    def _():
        h = _rms(x_ref[...], g_ref[...]) * (1.0 + sc_ref[0]) + sh_ref[0]
        hb = h.astype(BF16)
        h_scr[...] = hb
        fr_ref[...] = jnp.dot(hb, wf_ref[...], preferred_element_type=F32)
        ft_ref[...] = _dot_nt(wft_ref[...], hb)

    @pl.when(j < n_t_steps)
    def _():
        scale = jnp.where(j < n_q_steps, LOG2E * HEAD_DIM ** -0.5, 1.0).astype(F32)
        a = _dot_nt(wt_ref[...], h_scr[...]) * scale
        for u in range(heads_per_step):
            qvt_ref[u] = a[u * HEAD_DIM:(u + 1) * HEAD_DIM, :].astype(BF16)

    @pl.when(j >= n_t_steps)
    def _():
        a = jnp.dot(h_scr[...], w_ref[...], preferred_element_type=F32)

        @pl.when(j < n_t_steps + n_k_steps)
        def _():
            for u in range(heads_per_step):
                k_ref[u] = a[:, u * HEAD_DIM:(u + 1) * HEAD_DIM].astype(BF16)

        @pl.when(j >= n_t_steps + n_k_steps)
        def _():
            conv_ref[...] = a.astype(BF16)


def _inproj(x2, g1, mod, w_t, w_n, w_f, w_ft, *, seq, attn_width, conv_width, tm, tn):
    t, d = x2.shape
    nb = seq // tm
    n_q_steps = attn_width // tn
    n_t_steps = 2 * attn_width // tn
    n_k_steps = attn_width // tn
    n_c_steps = conv_width // tn
    n_steps = n_t_steps + n_k_steps + n_c_steps
    hps = tn // HEAD_DIM
    n_heads = attn_width // HEAD_DIM
    kern = functools.partial(_inproj_kernel, n_q_steps=n_q_steps, n_t_steps=n_t_steps,
                             n_k_steps=n_k_steps, heads_per_step=hps)
    return pl.pallas_call(
        kern,
        grid=(t // tm, n_steps),
        in_specs=[
            pl.BlockSpec((tm, d), lambda i, j: (i, 0)),
            pl.BlockSpec((1, d), lambda i, j: (0, 0)),
            pl.BlockSpec((1, 1, d), lambda i, j: (i // nb, 0, 0)),
            pl.BlockSpec((1, 1, d), lambda i, j: (i // nb, 0, 1)),
            pl.BlockSpec((tn, d), lambda i, j: (jnp.minimum(j, n_t_steps - 1), 0)),
            pl.BlockSpec((d, tn), lambda i, j: (0, jnp.maximum(j - n_t_steps, 0))),
            pl.BlockSpec((d, V7X_LANES), lambda i, j: (0, 0)),
            pl.BlockSpec((2 * V7X_SUBLANES, d), lambda i, j: (0, 0)),
        ],
        out_specs=[
            pl.BlockSpec((hps, HEAD_DIM, tm), lambda i, j: (jnp.minimum(j, n_t_steps - 1), 0, i)),
            pl.BlockSpec((hps, tm, HEAD_DIM),
                         lambda i, j: (jnp.clip(j - n_t_steps, 0, n_k_steps - 1), i, 0)),
            pl.BlockSpec((tm, tn),
                         lambda i, j: (i, jnp.clip(j - n_t_steps - n_k_steps, 0, n_c_steps - 1))),
            pl.BlockSpec((tm, V7X_LANES), lambda i, j: (i, 0)),
            pl.BlockSpec((2 * V7X_SUBLANES, tm), lambda i, j: (0, i)),
        ],
        out_shape=[
            jax.ShapeDtypeStruct((2 * n_heads, HEAD_DIM, t), BF16),
            jax.ShapeDtypeStruct((n_heads, t, HEAD_DIM), BF16),
            jax.ShapeDtypeStruct((t, conv_width), BF16),
            jax.ShapeDtypeStruct((t, V7X_LANES), F32),
            jax.ShapeDtypeStruct((2 * V7X_SUBLANES, t), F32),
        ],
        scratch_shapes=[pltpu.VMEM((tm, d), BF16)],
        compiler_params=_params(("arbitrary", "arbitrary")),
        name="norm_inproj",
    )(x2, g1, mod, mod, w_t, w_n, w_f, w_ft)


def _log_sigmoid(z):
    return jnp.minimum(z, 0.0) - jnp.log1p(jnp.exp(-jnp.abs(z)))


def _fcum_kernel(fr_ref, ft_ref, br_ref, bt_ref, kf_ref, qf_ref, car_r, car_t, *, nb, tc, n_heads):
    i = pl.program_id(0)

    @pl.when(i % nb == 0)
    def _():
        car_r[...] = jnp.zeros_like(car_r)
        car_t[...] = jnp.zeros_like(car_t)

    row = lax.broadcasted_iota(I32, (tc, tc), 0)
    col = lax.broadcasted_iota(I32, (tc, tc), 1)
    lower = (row >= col).astype(BF16)
    upper = (row <= col).astype(BF16)

    lf_r = _log_sigmoid(fr_ref[...] + br_ref[...])
    cum_r = car_r[0:1, :]
    for part in _split3(lf_r):
        cum_r = cum_r + jnp.dot(lower, part, preferred_element_type=F32)
    car_r[...] = jnp.broadcast_to(cum_r[tc - 1:tc, :], car_r.shape)

    lf_t = _log_sigmoid(ft_ref[...] + bt_ref[:, 0:1])
    cum_t = car_t[:, 0:1]
    for part in _split3(lf_t):
        cum_t = cum_t + jnp.dot(part, upper, preferred_element_type=F32)
    car_t[...] = jnp.broadcast_to(cum_t[:, tc - 1:tc], car_t.shape)

    lane = lax.broadcasted_iota(I32, (tc, V7X_LANES), 1)
    sub = lax.broadcasted_iota(I32, (F_ROWS, tc), 0)
    for h in range(n_heads):
        k_hi, k_mid, k_lo = _split3(jnp.broadcast_to(cum_r[:, h:h + 1] * LOG2E, (tc, V7X_LANES)))
        ones = jnp.where(lane < 6, 1.0, 0.0).astype(BF16)
        kf_ref[h] = jnp.where(lane == 0, k_hi, jnp.where(lane == 1, k_mid, jnp.where(lane == 2, k_lo, ones)))
        q_hi, q_mid, q_lo = _split3(jnp.broadcast_to(cum_t[h:h + 1, :] * LOG2E, (F_ROWS, tc)))
        neg = jnp.where(sub < 3, -1.0, 0.0).astype(BF16)
        qf_ref[h] = jnp.where(sub == 3, q_hi, jnp.where(sub == 4, q_mid, jnp.where(sub == 5, q_lo, neg)))


def _fcum(f_rows, f_t, bias_row, bias_col, *, seq, tc, n_heads):
    t = f_rows.shape[0]
    nb = seq // tc
    rows_t = f_t.shape[0]
    return pl.pallas_call(
        functools.partial(_fcum_kernel, nb=nb, tc=tc, n_heads=n_heads),
        grid=(t // tc,),
        in_specs=[
            pl.BlockSpec((tc, V7X_LANES), lambda i: (i, 0)),
            pl.BlockSpec((rows_t, tc), lambda i: (0, i)),
            pl.BlockSpec((1, V7X_LANES), lambda i: (0, 0)),
            pl.BlockSpec((rows_t, V7X_LANES), lambda i: (0, 0)),
        ],
        out_specs=[
            pl.BlockSpec((n_heads, tc, V7X_LANES), lambda i: (0, i, 0)),
            pl.BlockSpec((n_heads, F_ROWS, tc), lambda i: (0, 0, i)),
        ],
        out_shape=[
            jax.ShapeDtypeStruct((n_heads, t, V7X_LANES), BF16),
            jax.ShapeDtypeStruct((n_heads, F_ROWS, t), BF16),
        ],
        scratch_shapes=[pltpu.VMEM((V7X_SUBLANES, V7X_LANES), F32),
                        pltpu.VMEM((rows_t, V7X_LANES), F32)],
        compiler_params=_params(("arbitrary",)),
        name="forget_cumsum",
    )(f_rows, f_t, bias_row, bias_col)


def _attn_kernel(q_ref, qf_ref, k_ref, kf_ref, vt_ref, o_ref, qa_scr, acc_scr, s0, s1, p0, p1, st_scr,
                 *, tq):
    i = pl.program_id(2)
    qa_scr[0:HEAD_DIM, :] = q_ref[0]
    qa_scr[HEAD_DIM:HEAD_DIM + F_ROWS, :] = qf_ref[0]
    qa_scr[HEAD_DIM + F_ROWS:, :] = jnp.zeros((HEAD_DIM - F_ROWS, tq), BF16)
    acc_scr[...] = jnp.zeros_like(acc_scr)
    p1[...] = jnp.zeros_like(p1)
    st_scr[0:1, :] = jnp.full((1, tq), NEG_INF, F32)
    st_scr[1:2, :] = jnp.zeros((1, tq), F32)
    st_scr[2:3, :] = jnp.ones((1, tq), F32)

    def logits(j, s_out):
        start = pl.multiple_of(j * tq, tq)
        kk = jnp.concatenate([k_ref[0, pl.ds(start, tq), :], kf_ref[0, pl.ds(start, tq), :]], axis=1)
        s_out[...] = jnp.dot(kk, qa_scr[...], preferred_element_type=F32)

    def values(j, p_in):
        start = pl.multiple_of(jnp.maximum(j, 0) * tq, tq)
        vt = vt_ref[0, :, pl.ds(start, tq)]
        acc_scr[...] = st_scr[2:3, :] * acc_scr[...] + jnp.dot(vt, p_in[...], preferred_element_type=F32)

    def softmax(s_in, p_out, masked):
        s = s_in[...]
        if masked:
            key = lax.broadcasted_iota(I32, (tq, tq), 0)
            qry = lax.broadcasted_iota(I32, (tq, tq), 1)
            s = jnp.where(key <= qry, s, NEG_INF)
        m_prev = st_scr[0:1, :]
        m_new = jnp.maximum(m_prev, jnp.max(s, axis=0, keepdims=True))
        alpha = jnp.exp2(m_prev - m_new)
        p = jnp.exp2(s - m_new)
        st_scr[1:2, :] = alpha * st_scr[1:2, :] + jnp.sum(p, axis=0, keepdims=True)
        st_scr[0:1, :] = m_new
        st_scr[2:3, :] = alpha
        p_out[...] = p.astype(BF16)

    def stage(j, s_cur, s_nxt, p_cur, p_prev):
        logits(j + 1, s_nxt)
        values(j - 1, p_prev)
        softmax(s_cur, p_cur, False)

    def tail(s_cur, p_cur, p_prev):
        values(i - 1, p_prev)
        softmax(s_cur, p_cur, True)
        values(i, p_cur)
        o_ref[...] = (acc_scr[...] / st_scr[1:2, :]).T

    logits(0, s0)

    def pair(t, carry):
        stage(2 * t, s0, s1, p0, p1)
        stage(2 * t + 1, s1, s0, p1, p0)
        return carry

    lax.fori_loop(0, i // 2, pair, 0)

    @pl.when(i % 2 == 0)
    def _():
        tail(s0, p0, p1)

    @pl.when(i % 2 == 1)
    def _():
        stage(i - 1, s0, s1, p0, p1)
        tail(s1, p1, p0)


def _attention(qvt, qf, k, kf, *, batch, seq, n_heads, tq):
    t = batch * seq
    nq = seq // tq
    return pl.pallas_call(
        functools.partial(_attn_kernel, tq=tq),
        grid=(batch, n_heads, nq),
        in_specs=[
            pl.BlockSpec((1, HEAD_DIM, tq), lambda b, h, i: (h, 0, b * nq + i)),
            pl.BlockSpec((1, F_ROWS, tq), lambda b, h, i: (h, 0, b * nq + i)),
            pl.BlockSpec((1, seq, HEAD_DIM), lambda b, h, i: (h, b, 0)),
            pl.BlockSpec((1, seq, V7X_LANES), lambda b, h, i: (h, b, 0)),
            pl.BlockSpec((1, HEAD_DIM, seq), lambda b, h, i: (n_heads + h, 0, b)),
        ],
        out_specs=pl.BlockSpec((tq, HEAD_DIM), lambda b, h, i: (b * nq + i, h)),
        out_shape=jax.ShapeDtypeStruct((t, n_heads * HEAD_DIM), F32),
        scratch_shapes=[pltpu.VMEM((2 * HEAD_DIM, tq), BF16), pltpu.VMEM((HEAD_DIM, tq), F32),
                        pltpu.VMEM((tq, tq), F32), pltpu.VMEM((tq, tq), F32),
                        pltpu.VMEM((tq, tq), BF16), pltpu.VMEM((tq, tq), BF16),
                        pltpu.VMEM((V7X_SUBLANES, tq), F32)],
        compiler_params=_params(("arbitrary", "arbitrary", "arbitrary")),
        name="forget_attention",
    )(qvt, qf, k, kf, qvt)


def _mixer_out_kernel(x_ref, ya_ref, cv_ref, scw_ref, cfw_ref, cfb_ref, lng_ref, lnb_ref, ong_ref,
                      wout_ref, g1_ref, n2g_ref, sh2_ref, sc2_ref, wr_hi_ref, wr_lo_ref,
                      xo_ref, hp_ref, lg_ref, zbuf, ubuf, conv_scr, *, nb, tm, sc_w, cf_w, attn_w):
    i = pl.program_id(0)

    @pl.when(i % nb == 0)
    def _():
        zbuf[0:SC_HALO, :] = jnp.zeros((SC_HALO, sc_w), F32)
        ubuf[0:CF_HALO, :] = jnp.zeros((CF_HALO, cf_w), F32)
        ubuf[CF_HALO + tm:, :] = jnp.zeros((V7X_SUBLANES, cf_w), F32)

    cv = cv_ref[...].astype(F32)
    sc_b = cv[:, 0:sc_w]
    sc_c = cv[:, sc_w:2 * sc_w]
    sc_u = cv[:, 2 * sc_w:3 * sc_w]
    cf_a = cv[:, 3 * sc_w:3 * sc_w + cf_w]
    cf_gate = cv[:, 3 * sc_w + cf_w:3 * sc_w + 2 * cf_w]

    zbuf[SC_HALO:SC_HALO + tm, :] = sc_c * sc_u
    conv = jnp.zeros((tm, sc_w), F32)
    for k in range(SC_KERNEL):
        off = SC_HALO - (SC_KERNEL - 1) + k
        conv = conv + scw_ref[k:k + 1, :] * zbuf[off:off + tm, :]
    y_sc = sc_b * conv
    zbuf[0:SC_HALO, :] = zbuf[tm:tm + SC_HALO, :]

    ubuf[CF_HALO:CF_HALO + tm, :] = cf_a * jax.nn.sigmoid(cf_gate)
    def conv_chunk(c, carry):
        r0 = pl.multiple_of((c // (cf_w // V7X_LANES)) * CONV_ROWS, CONV_ROWS)
        c0 = pl.multiple_of((c % (cf_w // V7X_LANES)) * V7X_LANES, V7X_LANES)
        cols = pl.ds(c0, V7X_LANES)
        acc = jnp.zeros((CONV_ROWS, V7X_LANES), F32) + cfb_ref[:, cols]
        for phase in range(V7X_SUBLANES):
            z = None
            for k in range(CF_KERNEL):
                off = CF_HALO - (CF_KERNEL - 1) + k
                if off % V7X_SUBLANES != phase:
                    continue
                win = pl.ds(pl.multiple_of(r0 + (off - phase), V7X_SUBLANES), CONV_ROWS + V7X_SUBLANES)
                term = cfw_ref[k:k + 1, cols] * ubuf[win, cols]
                z = term if z is None else z + term
            if z is not None:
                acc = acc + z[phase:phase + CONV_ROWS, :]
        conv_scr[pl.ds(r0, CONV_ROWS), cols] = acc
        return carry

    lax.fori_loop(0, (tm // CONV_ROWS) * (cf_w // V7X_LANES), conv_chunk, 0)
    ubuf[0:CF_HALO, :] = ubuf[tm:tm + CF_HALO, :]
    ong = ong_ref[...]
    w_hi = wr_hi_ref[...]
    rg = tm // MIX_ROW_GROUPS
    for grp in range(MIX_ROW_GROUPS):
        rows = slice(grp * rg, (grp + 1) * rg)
        conv = conv_scr[rows, :]
        mu = jnp.mean(conv, axis=-1, keepdims=True)
        cen = conv - mu
        var = jnp.mean(cen * cen, axis=-1, keepdims=True)
        y_cf = _silu(cen * lax.rsqrt(var + NORM_EPS) * lng_ref[...] + lnb_ref[...])
        y = jnp.concatenate([
            _rms(ya_ref[rows, :], ong[:, 0:attn_w]),
            _rms(y_sc[rows, :], ong[:, attn_w:attn_w + sc_w]),
            _rms(y_cf, ong[:, attn_w + sc_w:attn_w + sc_w + cf_w]),
        ], axis=-1).astype(BF16)
        mix = jnp.dot(y, wout_ref[...], preferred_element_type=F32)
        x_new = x_ref[rows, :] + g1_ref[0] * mix
        xo_ref[rows, :] = x_new

        h2 = _rms(x_new, n2g_ref[...]) * (1.0 + sc2_ref[0]) + sh2_ref[0]
        half = h2.shape[-1] // 2
        hp_ref[rows, :] = _pack_bf16_pair(h2[:, :half], h2[:, half:])
        h_hi = h2.astype(BF16)
        h_lo = (h2 - h_hi.astype(F32)).astype(BF16)
        lg_ref[:, rows] = _dot_nt(w_hi, h_hi) + _dot_nt(w_hi, h_lo) + _dot_nt(wr_lo_ref[...], h_hi)


def _mixer_out(x2, y_attn, conv_in, scw, cfw, cfb, lng, lnb, ong, w_out, mod, n2g, wr_hi, wr_lo,
               *, seq, tm):
    t, d = x2.shape
    attn_w = y_attn.shape[1]
    sc_w = scw.shape[1]
    cf_w = cfw.shape[1]
    conv_w = conv_in.shape[1]
    nb = seq // tm
    n_exp = wr_hi.shape[0]
    full = lambda a: pl.BlockSpec(a.shape, lambda i: (0,) * a.ndim)
    modspec = lambda c: pl.BlockSpec((1, 1, d), lambda i: (i // nb, 0, c))
    kern = functools.partial(_mixer_out_kernel, nb=nb, tm=tm, sc_w=sc_w, cf_w=cf_w, attn_w=attn_w)
    return pl.pallas_call(
        kern,
        grid=(t // tm,),
        in_specs=[
            pl.BlockSpec((tm, d), lambda i: (i, 0)),
            pl.BlockSpec((tm, attn_w), lambda i: (i, 0)),
            pl.BlockSpec((tm, conv_w), lambda i: (i, 0)),
            full(scw), full(cfw), full(cfb), full(lng), full(lnb), full(ong), full(w_out),
            modspec(2), full(n2g), modspec(3), modspec(4), full(wr_hi), full(wr_lo),
        ],
        out_specs=[
            pl.BlockSpec((tm, d), lambda i: (i, 0)),
            pl.BlockSpec((tm, d // 2), lambda i: (i, 0)),
            pl.BlockSpec((n_exp, tm), lambda i: (0, i)),
        ],
        out_shape=[
            jax.ShapeDtypeStruct((t, d), F32),
            jax.ShapeDtypeStruct((t, d // 2), U32),
            jax.ShapeDtypeStruct((n_exp, t), F32),
        ],
        scratch_shapes=[pltpu.VMEM((SC_HALO + tm, sc_w), F32),
                        pltpu.VMEM((CF_HALO + tm + V7X_SUBLANES, cf_w), F32),
                        pltpu.VMEM((tm, cf_w), F32)],
        compiler_params=_params(("arbitrary",)),
        name="conv_norm_outproj",
    )(x2, y_attn, conv_in, scw, cfw, cfb, lng, lnb, ong, w_out, mod, n2g, mod, mod, wr_hi, wr_lo)


def _col_max(x):
    return jnp.max(x, axis=0, keepdims=True)


def _col_min(x):
    return jnp.min(x, axis=0, keepdims=True)


def _router_kernel(lg_ref, bias_ref, eidx_ref, gate_ref, rank_ref, cnt_ref, carry, *, tr):
    @pl.when(pl.program_id(0) == 0)
    def _():
        carry[...] = jnp.zeros_like(carry)

    scores = jax.nn.sigmoid(lg_ref[...])
    biased = scores + bias_ref[...]
    rid = lax.broadcasted_iota(I32, (GROUP_SIZE, tr), 0)
    sc_g = [scores[GROUP_SIZE * g:GROUP_SIZE * (g + 1)] for g in range(N_GROUPS)]
    bi_g = [biased[GROUP_SIZE * g:GROUP_SIZE * (g + 1)] for g in range(N_GROUPS)]

    gs = []
    for g in range(N_GROUPS):
        m1 = _col_max(bi_g[g])
        first = _col_min(jnp.where(bi_g[g] == m1, rid, GROUP_SIZE))
        m2 = _col_max(jnp.where(rid == first, NEG_INF, bi_g[g]))
        gs.append(m1 + m2)
    grp = jnp.concatenate(gs, axis=0)

    keep = jnp.zeros((N_GROUPS, tr), F32)
    for _ in range(TOPK_GROUPS):
        m = _col_max(grp)
        first = _col_min(jnp.where(grp == m, rid, N_GROUPS))
        hit = rid == first
        keep = jnp.where(hit, 1.0, keep)
        grp = jnp.where(hit, NEG_INF, grp)

    cand = [jnp.where(keep[g:g + 1, :] > 0.5, bi_g[g], NEG_INF) for g in range(N_GROUPS)]
    eid_g = [rid + GROUP_SIZE * g for g in range(N_GROUPS)]
    onehot = [jnp.zeros((GROUP_SIZE, tr), F32) for _ in range(N_GROUPS)]
    e_sel, g_sel = [], []
    for _ in range(TOP_K):
        m = cand[0]
        for g in range(1, N_GROUPS):
            m = jnp.maximum(m, cand[g])
        m = _col_max(m)
        first = jnp.where(cand[0] == m, eid_g[0], N_EXPERTS)
        for g in range(1, N_GROUPS):
            first = jnp.minimum(first, jnp.where(cand[g] == m, eid_g[g], N_EXPERTS))
        first = _col_min(first)
        gate = jnp.zeros((GROUP_SIZE, tr), F32)
        for g in range(N_GROUPS):
            hit = eid_g[g] == first
            gate = gate + jnp.where(hit, sc_g[g], 0.0)
            cand[g] = jnp.where(hit, NEG_INF, cand[g])
            onehot[g] = jnp.where(hit, 1.0, onehot[g])
        e_sel.append(first)
        g_sel.append(jnp.sum(gate, axis=0, keepdims=True))

    gates = jnp.concatenate(g_sel, axis=0)
    gates = gates / (jnp.sum(gates, axis=0, keepdims=True) + 1e-20) * ROUTED_SCALE
    eidx = jnp.concatenate(e_sel, axis=0)
    eidx_ref[...] = eidx
    gate_ref[...] = gates

    oh = jnp.concatenate(onehot, axis=0)
    srow = lax.broadcasted_iota(I32, (tr, tr), 0)
    scol = lax.broadcasted_iota(I32, (tr, tr), 1)
    strict = (srow < scol).astype(BF16)
    before = jnp.dot(oh.astype(BF16), strict, preferred_element_type=F32) + carry[:, 0:1]
    ranks = []
    for k in range(TOP_K):
        acc = jnp.zeros((GROUP_SIZE, tr), F32)
        for g in range(N_GROUPS):
            acc = acc + jnp.where(eid_g[g] == e_sel[k], before[GROUP_SIZE * g:GROUP_SIZE * (g + 1)], 0.0)
        ranks.append(jnp.sum(acc, axis=0, keepdims=True))
    rank_ref[...] = jnp.concatenate(ranks, axis=0).astype(I32)
    carry[...] = carry[...] + jnp.sum(oh, axis=1, keepdims=True)
    cnt_ref[...] = carry[...].astype(I32)


def _router(logits_t, bias_b, *, tr):
    n_exp, t = logits_t.shape
    return pl.pallas_call(
        functools.partial(_router_kernel, tr=tr),
        grid=(t // tr,),
        in_specs=[
            pl.BlockSpec((n_exp, tr), lambda i: (0, i)),
            pl.BlockSpec((n_exp, tr), lambda i: (0, 0)),
        ],
        out_specs=[
            pl.BlockSpec((TOP_K, tr), lambda i: (0, i)),
            pl.BlockSpec((TOP_K, tr), lambda i: (0, i)),
            pl.BlockSpec((TOP_K, tr), lambda i: (0, i)),
            pl.BlockSpec((n_exp, V7X_LANES), lambda i: (0, 0)),
        ],
        out_shape=[
            jax.ShapeDtypeStruct((TOP_K, t), I32),
            jax.ShapeDtypeStruct((TOP_K, t), F32),
            jax.ShapeDtypeStruct((TOP_K, t), I32),
            jax.ShapeDtypeStruct((n_exp, V7X_LANES), I32),
        ],
        scratch_shapes=[pltpu.VMEM((n_exp, V7X_LANES), F32)],
        compiler_params=_params(("arbitrary",)),
        name="router_select",
    )(logits_t, bias_b)


def _dest_kernel(offs_ref, eidx_ref, rank_ref, dest_ref):
    eidx = eidx_ref[...]
    base = jnp.zeros(eidx.shape, I32)
    for e in range(N_EXPERTS):
        base = jnp.where(eidx == e, offs_ref[e], base)
    dest_ref[...] = base + rank_ref[...]


def _dest(offs, eidx_t, rank_t, *, tr):
    t = eidx_t.shape[1]
    return pl.pallas_call(
        _dest_kernel,
        grid_spec=pltpu.PrefetchScalarGridSpec(
            num_scalar_prefetch=1,
            grid=(t // tr,),
            in_specs=[pl.BlockSpec((TOP_K, tr), lambda i, o: (0, i)),
                      pl.BlockSpec((TOP_K, tr), lambda i, o: (0, i))],
            out_specs=pl.BlockSpec((TOP_K, tr), lambda i, o: (0, i)),
        ),
        out_shape=jax.ShapeDtypeStruct((TOP_K, t), I32),
        compiler_params=_params(("arbitrary",)),
        name="dispatch_rows",
    )(offs, eidx_t, rank_t)


def _expert_kernel(be_ref, used_ref, src0_ref, srcn_ref, dstp_ref, dstc_ref, h_ref, wg_ref, wu_ref, wd_ref,
                   out_ref, wg_s, wu_s, wd_s, xbuf, ybuf, gsem, ssem, *, spare_row):
    n = pl.program_id(0)
    used = used_ref[0]
    slot = n % 2
    other = 1 - slot

    def gather(src_ref, s):
        for r in range(EXPERT_BLOCK):
            pltpu.make_async_copy(h_ref.at[src_ref[0, 0, r]], xbuf.at[s, pl.ds(r, 1)],
                                  gsem.at[s]).start()

    def scatter(dst_ref, s):
        for r in range(EXPERT_BLOCK):
            pltpu.make_async_copy(ybuf.at[s, pl.ds(r, 1)], out_ref.at[pl.ds(dst_ref[0, 0, r], 1)],
                                  ssem.at[s]).start()

    def gather_wait(s):
        pltpu.make_async_copy(out_ref.at[pl.ds(0, EXPERT_BLOCK)], xbuf.at[s], gsem.at[s]).wait()

    def scatter_wait(s):
        pltpu.make_async_copy(ybuf.at[s], out_ref.at[pl.ds(0, EXPERT_BLOCK)], ssem.at[s]).wait()

    @pl.when(n == 0)
    def _():
        gather(src0_ref, 0)
        ybuf[...] = jnp.zeros_like(ybuf)
        pltpu.make_async_copy(ybuf.at[0], out_ref.at[pl.ds(spare_row, EXPERT_BLOCK)], ssem.at[0]).start()

    @pl.when((n < used) & ((n == 0) | (be_ref[n] != be_ref[jnp.maximum(n - 1, 0)])))
    def _():
        wg_s[...] = wg_ref[0].astype(BF16)
        wu_s[...] = wu_ref[0].astype(BF16)
        wd_s[...] = wd_ref[0].astype(BF16)

    def step(cur, oth):
        gather_wait(cur)
        scatter_wait(cur)
        gather(srcn_ref, oth)
        scatter(dstp_ref, oth)
        lo, hi = _unpack_bf16_pair(xbuf[cur])
        lo = lo.astype(BF16)
        hi = hi.astype(BF16)
        half = lo.shape[-1]
        g = (jnp.dot(lo, wg_s[:half, :], preferred_element_type=F32)
             + jnp.dot(hi, wg_s[half:, :], preferred_element_type=F32))
        u = (jnp.dot(lo, wu_s[:half, :], preferred_element_type=F32)
             + jnp.dot(hi, wu_s[half:, :], preferred_element_type=F32))
        hid = (_silu(g) * u).astype(BF16)
        y = jnp.dot(hid, wd_s[...], preferred_element_type=F32)
        ybuf[cur] = _pack_bf16_pair(y[:, :half], y[:, half:])

    @pl.when((n < used) & (slot == 0))
    def _():
        step(0, 1)

    @pl.when((n < used) & (slot == 1))
    def _():
        step(1, 0)

    @pl.when(n == used - 1)
    def _():
        scatter(dstc_ref, slot)
        scatter_wait(other)
        scatter_wait(slot)
        gather_wait(other)


def _experts(block_expert, used, row_src, row_dst, h2p, wg, wu, wd, *, layer, n_out_rows, spare_row):
    n_blocks = row_src.shape[0]
    half = h2p.shape[-1]
    _, _, d, hid = wg.shape
    last = lambda n, used: jnp.minimum(n, used[0] - 1)
    wsel = lambda n, be, used: (layer, be[last(n, used)], 0, 0)
    smem = lambda imap: pl.BlockSpec((1, 1, EXPERT_BLOCK), imap, memory_space=pltpu.SMEM)
    return pl.pallas_call(
        functools.partial(_expert_kernel, spare_row=spare_row),
        grid_spec=pltpu.PrefetchScalarGridSpec(
            num_scalar_prefetch=2,
            grid=(n_blocks,),
            in_specs=[
                smem(lambda n, be, used: (0, 0, 0)),
                smem(lambda n, be, used: (jnp.minimum(n + 1, used[0] - 1), 0, 0)),
                smem(lambda n, be, used: (last(n, used), 0, 0)),
                smem(lambda n, be, used: (last(n, used) + 1, 0, 0)),
                pl.BlockSpec(memory_space=pl.ANY),
                pl.BlockSpec((None, 1, d, hid), wsel),
                pl.BlockSpec((None, 1, d, hid), wsel),
                pl.BlockSpec((None, 1, hid, d), wsel),
            ],
            out_specs=pl.BlockSpec(memory_space=pl.ANY),
            scratch_shapes=[pltpu.VMEM((d, hid), BF16), pltpu.VMEM((d, hid), BF16),
                            pltpu.VMEM((hid, d), BF16),
                            pltpu.VMEM((2, EXPERT_BLOCK, half), U32), pltpu.VMEM((2, EXPERT_BLOCK, half), U32),
                            pltpu.SemaphoreType.DMA((2,)), pltpu.SemaphoreType.DMA((2,))],
        ),
        out_shape=jax.ShapeDtypeStruct((n_out_rows, half), U32),
        compiler_params=_params(("arbitrary",)),
        name="expert_ffn",
    )(block_expert, used, row_src, row_src, row_dst, row_dst, h2p, wg, wu, wd)


def _combine_kernel(x_ref, hp_ref, gate_ref, *rest, td, final_norm):
    y_refs = rest[:TOP_K]
    wsg_ref, wsu_ref, wsd_ref, g2_ref, fng_ref, o_ref = rest[TOP_K:]
    lo, hi = _unpack_bf16_pair(hp_ref[...])
    lo = lo.astype(BF16)
    hi = hi.astype(BF16)
    half = lo.shape[-1]
    g = (jnp.dot(lo, wsg_ref[:half, :], preferred_element_type=F32)
         + jnp.dot(hi, wsg_ref[half:, :], preferred_element_type=F32))
    u = (jnp.dot(lo, wsu_ref[:half, :], preferred_element_type=F32)
         + jnp.dot(hi, wsu_ref[half:, :], preferred_element_type=F32))
    hid = (_silu(g) * u).astype(BF16)
    shared = jnp.dot(hid, wsd_ref[...], preferred_element_type=F32)

    gate = gate_ref[...]
    r_lo = jnp.zeros((td, half), F32)
    r_hi = jnp.zeros((td, half), F32)
    for k in range(TOP_K):
        y_lo, y_hi = _unpack_bf16_pair(y_refs[k][...])
        w = gate[:, k:k + 1]
        r_lo = r_lo + w * y_lo
        r_hi = r_hi + w * y_hi
    routed = jnp.concatenate([r_lo, r_hi], axis=-1)
    out = x_ref[...] + g2_ref[0] * (routed + shared)
    if final_norm:
        out = _rms(out, fng_ref[...])
    o_ref[...] = out


def _combine(x2, h2p, gate_rows, y8, wsg, wsu, wsd, mod, fng, *, seq, td, final_norm):
    t, d = x2.shape
    half = d // 2
    nb = seq // td
    n_tb = t // td
    full = lambda a: pl.BlockSpec(a.shape, lambda i: (0,) * a.ndim)
    y_specs = [pl.BlockSpec((td, half), functools.partial(lambda i, k: (k * n_tb + i, 0), k=k))
               for k in range(TOP_K)]
    return pl.pallas_call(
        functools.partial(_combine_kernel, td=td, final_norm=final_norm),
        grid=(n_tb,),
        in_specs=[
            pl.BlockSpec((td, d), lambda i: (i, 0)),
            pl.BlockSpec((td, half), lambda i: (i, 0)),
            pl.BlockSpec((td, TOP_K), lambda i: (i, 0)),
            *y_specs,
            full(wsg), full(wsu), full(wsd),
            pl.BlockSpec((1, 1, d), lambda i: (i // nb, 0, 5)),
            full(fng),
        ],
        out_specs=pl.BlockSpec((td, d), lambda i: (i, 0)),
        out_shape=jax.ShapeDtypeStruct((t, d), F32),
        compiler_params=_params(("arbitrary",)),
        name="shared_combine",
    )(x2, h2p, gate_rows, *([y8] * TOP_K), wsg, wsu, wsd, mod, fng)


def kernel(x, c, w_ada, b_ada, norm1_g, norm2_g, w_in, b_forget, sc_conv_w, cf_conv_w, cf_conv_b,
           cf_ln_g, cf_ln_b, out_norm_g, w_out, w_router, router_bias, w_gate, w_up, w_down,
           ws_gate, ws_up, ws_down, final_norm_g):
    batch, seq, d = x.shape
    depth = w_ada.shape[0]
    t = batch * seq
    n_heads = b_forget.shape[1]
    attn_w = n_heads * HEAD_DIM
    sc_w = sc_conv_w.shape[2]
    cf_w = cf_conv_w.shape[2]
    conv_w = 3 * sc_w + 2 * cf_w
    assert w_in.shape[2] == 3 * attn_w + n_heads + conv_w
    assert batch <= V7X_SUBLANES and n_heads <= V7X_SUBLANES
    tl = _tiles(seq)
    n_assign = t * TOP_K
    n_blocks = -(-(n_assign + N_EXPERTS * (EXPERT_BLOCK - 1)) // EXPERT_BLOCK)
    n_rows = n_blocks * EXPERT_BLOCK

    c_pad = jnp.pad(c, ((0, V7X_SUBLANES - batch), (0, 0)))
    mod_all = _ada(c_pad, w_ada, b_ada)

    x2 = x.reshape(t, d)
    for l in range(depth):
        mod = mod_all[l, :batch].reshape(batch, 1, N_MOD * d)
        wl = w_in[l]
        w_t = jnp.concatenate([wl[:, :attn_w], wl[:, 2 * attn_w:3 * attn_w]], axis=1).T.astype(BF16)
        w_n = jnp.concatenate([wl[:, attn_w:2 * attn_w], wl[:, 3 * attn_w + n_heads:]], axis=1).astype(BF16)
        w_fcols = wl[:, 3 * attn_w:3 * attn_w + n_heads]
        w_f = jnp.pad(w_fcols, ((0, 0), (0, V7X_LANES - n_heads))).astype(BF16)
        w_ft = jnp.pad(w_fcols.T, ((0, 2 * V7X_SUBLANES - n_heads), (0, 0))).astype(BF16)
        qvt, k, conv_in, f_rows, f_t = _inproj(
            x2, norm1_g[l].reshape(1, d), mod, w_t, w_n, w_f, w_ft,
            seq=seq, attn_width=attn_w, conv_width=conv_w, tm=tl["tm_in"], tn=tl["tn_in"])

        bias_row = jnp.pad(b_forget[l], (0, V7X_LANES - n_heads)).reshape(1, V7X_LANES)
        bias_col = jnp.broadcast_to(
            jnp.pad(b_forget[l], (0, 2 * V7X_SUBLANES - n_heads))[:, None], (2 * V7X_SUBLANES, V7X_LANES))
        kf, qf = _fcum(f_rows, f_t, bias_row, bias_col, seq=seq, tc=tl["tc"], n_heads=n_heads)

        y_attn = _attention(qvt, qf, k, kf, batch=batch, seq=seq, n_heads=n_heads, tq=tl["tq"])

        scw = jnp.pad(sc_conv_w[l], ((0, V7X_SUBLANES - SC_KERNEL), (0, 0)))
        cfw = jnp.pad(cf_conv_w[l], ((0, CF_HALO - CF_KERNEL), (0, 0)))
        wr_t = w_router[l].T
        wr_hi = wr_t.astype(BF16)
        wr_lo = (wr_t - wr_hi.astype(F32)).astype(BF16)
        x2, h2p, logits_t = _mixer_out(
            x2, y_attn, conv_in, scw, cfw, cf_conv_b[l].reshape(1, cf_w), cf_ln_g[l].reshape(1, cf_w),
            cf_ln_b[l].reshape(1, cf_w), out_norm_g[l].reshape(1, d), w_out[l].astype(BF16), mod,
            norm2_g[l].reshape(1, d), wr_hi, wr_lo, seq=seq, tm=tl["tm_mix"])

        bias_b = jnp.broadcast_to(router_bias[l][:, None], (N_EXPERTS, tl["tr"]))
        eidx_t, gate_t, rank_t, counts = _router(logits_t, bias_b, tr=tl["tr"])
        counts = counts[:, 0]
        padded = (counts + EXPERT_BLOCK - 1) // EXPERT_BLOCK * EXPERT_BLOCK
        pad_end = jnp.cumsum(padded).astype(I32)
        offs = pad_end - padded
        dest_t = _dest(offs, eidx_t, rank_t, tr=tl["tr"])
        block_start = jnp.arange(n_blocks, dtype=I32) * EXPERT_BLOCK
        block_expert = jnp.minimum(
            jnp.sum((pad_end[None, :] <= block_start[:, None]).astype(I32), axis=1), N_EXPERTS - 1)
        used = (pad_end[-1:] // EXPERT_BLOCK).astype(I32)

        row_assign = jnp.full((n_rows,), -1, I32).at[dest_t.reshape(-1)].set(
            jnp.arange(n_assign, dtype=I32), unique_indices=True)
        is_real = row_assign >= 0
        row_src = jnp.where(is_real, row_assign % t, 0).reshape(n_blocks, 1, EXPERT_BLOCK)
        row_dst = jnp.where(is_real, row_assign, n_assign + jnp.arange(n_rows, dtype=I32))
        spare_block = n_assign + n_rows + jnp.arange(EXPERT_BLOCK, dtype=I32)
        row_dst = jnp.concatenate([spare_block, row_dst]).reshape(n_blocks + 1, 1, EXPERT_BLOCK)

        y8 = _experts(block_expert, used, row_src, row_dst, h2p.reshape(t, 1, d // 2), w_gate, w_up, w_down,
                      layer=l,
                      n_out_rows=n_assign + n_rows + 2 * EXPERT_BLOCK,
                      spare_row=n_assign + n_rows + EXPERT_BLOCK)
        x2 = _combine(x2, h2p, gate_t.T, y8, ws_gate[l].astype(BF16), ws_up[l].astype(BF16),
                      ws_down[l].astype(BF16), mod, final_norm_g.reshape(1, d),
                      seq=seq, td=tl["td"], final_norm=(l == depth - 1))
    return x2.reshape(batch, seq, d)
```

```python
import functools
import math

import jax
import jax.numpy as jnp
from jax import lax
from jax.experimental import pallas as pl
from jax.experimental.pallas import tpu as pltpu

F32 = jnp.float32
BF16 = jnp.bfloat16
I32 = jnp.int32
U32 = jnp.uint32

V7X_LANES = 128
V7X_SUBLANES = 8
V7X_VMEM_BYTES = 64 * 1024 * 1024
VMEM_LIMIT = V7X_VMEM_BYTES - 8 * 1024 * 1024

HEAD_DIM = 128
N_EXPERTS = 64
TOP_K = 8
N_GROUPS = 8
GROUP_SIZE = N_EXPERTS // N_GROUPS
TOPK_GROUPS = 4
ROUTED_SCALE = 2.5
EXPERT_BLOCK = 256
SC_KERNEL = 3
CF_KERNEL = 31
N_MOD = 6
NORM_EPS = 1e-6
SC_HALO = 8
CF_HALO = 32
NEG_INF = float("-inf")
LOG2E = math.log2(math.e)
F_ROWS = 16
ATTN_UNROLL = 4
CONV_ROWS = 64
MIX_ROW_GROUPS = 2


def _params(semantics, **kw):
    return pltpu.CompilerParams(dimension_semantics=semantics, vmem_limit_bytes=VMEM_LIMIT, **kw)


def _tiles(seq):
    return dict(
        tm_in=min(1024, seq),
        tn_in=512,
        tc=min(512, seq),
        tq=min(512, seq),
        tm_mix=min(256, seq),
        tr=min(512, seq),
        td=min(256, seq),
    )


def _rms(x, g):
    return x * lax.rsqrt(jnp.mean(x * x, axis=-1, keepdims=True) + NORM_EPS) * g


def _silu(x):
    return x * jax.nn.sigmoid(x)


def _pack_bf16_pair(lo, hi):
    lo_b = lax.bitcast_convert_type(lo.astype(BF16).astype(F32), U32) >> 16
    hi_b = lax.bitcast_convert_type(hi.astype(BF16).astype(F32), U32) & jnp.uint32(0xFFFF0000)
    return hi_b | lo_b


def _unpack_bf16_pair(p):
    lo = lax.bitcast_convert_type(p << 16, F32)
    hi = lax.bitcast_convert_type(p & jnp.uint32(0xFFFF0000), F32)
    return lo, hi


def _split3(x):
    a = x.astype(BF16)
    r = x - a.astype(F32)
    b = r.astype(BF16)
    c = (r - b.astype(F32)).astype(BF16)
    return a, b, c


def _dot_nt(a, b):
    return lax.dot_general(a, b, (((1,), (1,)), ((), ())), preferred_element_type=F32)


def _ada_kernel(c_ref, w_ref, b_ref, o_ref):
    c = c_ref[...]
    ca = _silu(c).astype(BF16)
    o_ref[0] = jnp.dot(ca, w_ref[0].astype(BF16), preferred_element_type=F32) + b_ref[0]


def _ada(c_pad, w_ada, b_ada):
    depth, d, n = w_ada.shape
    tn = 1024
    return pl.pallas_call(
        _ada_kernel,
        grid=(depth, n // tn),
        in_specs=[
            pl.BlockSpec((V7X_SUBLANES, d), lambda l, j: (0, 0)),
            pl.BlockSpec((1, d, tn), lambda l, j: (l, 0, j)),
            pl.BlockSpec((1, 1, tn), lambda l, j: (l, 0, j)),
        ],
        out_specs=pl.BlockSpec((1, V7X_SUBLANES, tn), lambda l, j: (l, 0, j)),
        out_shape=jax.ShapeDtypeStruct((depth, V7X_SUBLANES, n), F32),
        compiler_params=_params(("arbitrary", "arbitrary")),
        name="ada_mod",
    )(c_pad, w_ada, b_ada.reshape(depth, 1, n))


def _inproj_kernel(x_ref, g_ref, sh_ref, sc_ref, wt_ref, w_ref, wf_ref, wft_ref,
                   qvt_ref, k_ref, conv_ref, fr_ref, ft_ref, h_scr,
                   *, n_q_steps, n_t_steps, n_k_steps, heads_per_step):
    j = pl.program_id(1)

    @pl.when(j == 0)
    def _():
        h = _rms(x_ref[...], g_ref[...]) * (1.0 + sc_ref[0]) + sh_ref[0]
        hb = h.astype(BF16)
        h_scr[...] = hb
        fr_ref[...] = jnp.dot(hb, wf_ref[...], preferred_element_type=F32)
        ft_ref[...] = _dot_nt(wft_ref[...], hb)

    @pl.when(j < n_t_steps)
    def _():
        scale = jnp.where(j < n_q_steps, LOG2E * HEAD_DIM ** -0.5, 1.0).astype(F32)
        a = _dot_nt(wt_ref[...], h_scr[...]) * scale
        for u in range(heads_per_step):
            qvt_ref[u] = a[u * HEAD_DIM:(u + 1) * HEAD_DIM, :].astype(BF16)

    @pl.when(j >= n_t_steps)
    def _():
        a = jnp.dot(h_scr[...], w_ref[...], preferred_element_type=F32)

        @pl.when(j < n_t_steps + n_k_steps)
        def _():
            for u in range(heads_per_step):
                k_ref[u] = a[:, u * HEAD_DIM:(u + 1) * HEAD_DIM].astype(BF16)

        @pl.when(j >= n_t_steps + n_k_steps)
        def _():
            conv_ref[...] = a.astype(BF16)


def _inproj(x2, g1, mod, w_t, w_n, w_f, w_ft, *, seq, attn_width, conv_width, tm, tn):
    t, d = x2.shape
    nb = seq // tm
    n_q_steps = attn_width // tn
    n_t_steps = 2 * attn_width // tn
    n_k_steps = attn_width // tn
    n_c_steps = conv_width // tn
    n_steps = n_t_steps + n_k_steps + n_c_steps
    hps = tn // HEAD_DIM
    n_heads = attn_width // HEAD_DIM
    kern = functools.partial(_inproj_kernel, n_q_steps=n_q_steps, n_t_steps=n_t_steps,
                             n_k_steps=n_k_steps, heads_per_step=hps)
    return pl.pallas_call(
        kern,
        grid=(t // tm, n_steps),
        in_specs=[
            pl.BlockSpec((tm, d), lambda i, j: (i, 0)),
            pl.BlockSpec((1, d), lambda i, j: (0, 0)),
            pl.BlockSpec((1, 1, d), lambda i, j: (i // nb, 0, 0)),
            pl.BlockSpec((1, 1, d), lambda i, j: (i // nb, 0, 1)),
            pl.BlockSpec((tn, d), lambda i, j: (jnp.minimum(j, n_t_steps - 1), 0)),
            pl.BlockSpec((d, tn), lambda i, j: (0, jnp.maximum(j - n_t_steps, 0))),
            pl.BlockSpec((d, V7X_LANES), lambda i, j: (0, 0)),
            pl.BlockSpec((2 * V7X_SUBLANES, d), lambda i, j: (0, 0)),
        ],
        out_specs=[
            pl.BlockSpec((hps, HEAD_DIM, tm), lambda i, j: (jnp.minimum(j, n_t_steps - 1), 0, i)),
            pl.BlockSpec((hps, tm, HEAD_DIM),
                         lambda i, j: (jnp.clip(j - n_t_steps, 0, n_k_steps - 1), i, 0)),
            pl.BlockSpec((tm, tn),
                         lambda i, j: (i, jnp.clip(j - n_t_steps - n_k_steps, 0, n_c_steps - 1))),
            pl.BlockSpec((tm, V7X_LANES), lambda i, j: (i, 0)),
            pl.BlockSpec((2 * V7X_SUBLANES, tm), lambda i, j: (0, i)),
        ],
        out_shape=[
            jax.ShapeDtypeStruct((2 * n_heads, HEAD_DIM, t), BF16),
            jax.ShapeDtypeStruct((n_heads, t, HEAD_DIM), BF16),
            jax.ShapeDtypeStruct((t, conv_width), BF16),
            jax.ShapeDtypeStruct((t, V7X_LANES), F32),
            jax.ShapeDtypeStruct((2 * V7X_SUBLANES, t), F32),
        ],
        scratch_shapes=[pltpu.VMEM((tm, d), BF16)],
        compiler_params=_params(("arbitrary", "arbitrary")),
        name="norm_inproj",
    )(x2, g1, mod, mod, w_t, w_n, w_f, w_ft)


def _log_sigmoid(z):
    return jnp.minimum(z, 0.0) - jnp.log1p(jnp.exp(-jnp.abs(z)))


def _fcum_kernel(fr_ref, ft_ref, br_ref, bt_ref, kf_ref, qf_ref, car_r, car_t, *, nb, tc, n_heads):
    i = pl.program_id(0)

    @pl.when(i % nb == 0)
    def _():
        car_r[...] = jnp.zeros_like(car_r)
        car_t[...] = jnp.zeros_like(car_t)

    row = lax.broadcasted_iota(I32, (tc, tc), 0)
    col = lax.broadcasted_iota(I32, (tc, tc), 1)
    lower = (row >= col).astype(BF16)
    upper = (row <= col).astype(BF16)

    lf_r = _log_sigmoid(fr_ref[...] + br_ref[...])
    cum_r = car_r[0:1, :]
    for part in _split3(lf_r):
        cum_r = cum_r + jnp.dot(lower, part, preferred_element_type=F32)
    car_r[...] = jnp.broadcast_to(cum_r[tc - 1:tc, :], car_r.shape)

    lf_t = _log_sigmoid(ft_ref[...] + bt_ref[:, 0:1])
    cum_t = car_t[:, 0:1]
    for part in _split3(lf_t):
        cum_t = cum_t + jnp.dot(part, upper, preferred_element_type=F32)
    car_t[...] = jnp.broadcast_to(cum_t[:, tc - 1:tc], car_t.shape)

    lane = lax.broadcasted_iota(I32, (tc, V7X_LANES), 1)
    sub = lax.broadcasted_iota(I32, (F_ROWS, tc), 0)
    for h in range(n_heads):
        k_hi, k_mid, k_lo = _split3(jnp.broadcast_to(cum_r[:, h:h + 1] * LOG2E, (tc, V7X_LANES)))
        ones = jnp.where(lane < 6, 1.0, 0.0).astype(BF16)
        kf_ref[h] = jnp.where(lane == 0, k_hi, jnp.where(lane == 1, k_mid, jnp.where(lane == 2, k_lo, ones)))
        q_hi, q_mid, q_lo = _split3(jnp.broadcast_to(cum_t[h:h + 1, :] * LOG2E, (F_ROWS, tc)))
        neg = jnp.where(sub < 3, -1.0, 0.0).astype(BF16)
        qf_ref[h] = jnp.where(sub == 3, q_hi, jnp.where(sub == 4, q_mid, jnp.where(sub == 5, q_lo, neg)))


def _fcum(f_rows, f_t, bias_row, bias_col, *, seq, tc, n_heads):
    t = f_rows.shape[0]
    nb = seq // tc
    rows_t = f_t.shape[0]
    return pl.pallas_call(
        functools.partial(_fcum_kernel, nb=nb, tc=tc, n_heads=n_heads),
        grid=(t // tc,),
        in_specs=[
            pl.BlockSpec((tc, V7X_LANES), lambda i: (i, 0)),
            pl.BlockSpec((rows_t, tc), lambda i: (0, i)),
            pl.BlockSpec((1, V7X_LANES), lambda i: (0, 0)),
            pl.BlockSpec((rows_t, V7X_LANES), lambda i: (0, 0)),
        ],
        out_specs=[
            pl.BlockSpec((n_heads, tc, V7X_LANES), lambda i: (0, i, 0)),
            pl.BlockSpec((n_heads, F_ROWS, tc), lambda i: (0, 0, i)),
        ],
        out_shape=[
            jax.ShapeDtypeStruct((n_heads, t, V7X_LANES), BF16),
            jax.ShapeDtypeStruct((n_heads, F_ROWS, t), BF16),
        ],
        scratch_shapes=[pltpu.VMEM((V7X_SUBLANES, V7X_LANES), F32),
                        pltpu.VMEM((rows_t, V7X_LANES), F32)],
        compiler_params=_params(("arbitrary",)),
        name="forget_cumsum",
    )(f_rows, f_t, bias_row, bias_col)


def _attn_kernel(q_ref, qf_ref, k_ref, kf_ref, vt_ref, o_ref, qa_scr, acc_scr, s0, s1, p0, p1, st_scr,
                 *, tq):
    i = pl.program_id(2)
    qa_scr[0:HEAD_DIM, :] = q_ref[0]
    qa_scr[HEAD_DIM:HEAD_DIM + F_ROWS, :] = qf_ref[0]
    qa_scr[HEAD_DIM + F_ROWS:, :] = jnp.zeros((HEAD_DIM - F_ROWS, tq), BF16)
    acc_scr[...] = jnp.zeros_like(acc_scr)
    p1[...] = jnp.zeros_like(p1)
    st_scr[0:1, :] = jnp.full((1, tq), NEG_INF, F32)
    st_scr[1:2, :] = jnp.zeros((1, tq), F32)
    st_scr[2:3, :] = jnp.ones((1, tq), F32)

    def logits(j, s_out):
        start = pl.multiple_of(j * tq, tq)
        kk = jnp.concatenate([k_ref[0, pl.ds(start, tq), :], kf_ref[0, pl.ds(start, tq), :]], axis=1)
        s_out[...] = jnp.dot(kk, qa_scr[...], preferred_element_type=F32)

    def values(j, p_in):
        start = pl.multiple_of(jnp.maximum(j, 0) * tq, tq)
        vt = vt_ref[0, :, pl.ds(start, tq)]
        acc_scr[...] = st_scr[2:3, :] * acc_scr[...] + jnp.dot(vt, p_in[...], preferred_element_type=F32)

    def softmax(s_in, p_out, masked):
        s = s_in[...]
        if masked:
            key = lax.broadcasted_iota(I32, (tq, tq), 0)
            qry = lax.broadcasted_iota(I32, (tq, tq), 1)
            s = jnp.where(key <= qry, s, NEG_INF)
        m_prev = st_scr[0:1, :]
        m_new = jnp.maximum(m_prev, jnp.max(s, axis=0, keepdims=True))
        alpha = jnp.exp2(m_prev - m_new)
        p = jnp.exp2(s - m_new)
        st_scr[1:2, :] = alpha * st_scr[1:2, :] + jnp.sum(p, axis=0, keepdims=True)
        st_scr[0:1, :] = m_new
        st_scr[2:3, :] = alpha
        p_out[...] = p.astype(BF16)

    def stage(j, s_cur, s_nxt, p_cur, p_prev):
        values(j - 1, p_prev)
        logits(j + 1, s_nxt)
        softmax(s_cur, p_cur, False)

    def tail(s_cur, p_cur, p_prev):
        values(i - 1, p_prev)
        softmax(s_cur, p_cur, True)
        values(i, p_cur)
        o_ref[...] = (acc_scr[...] / st_scr[1:2, :]).T

    logits(0, s0)

    bufs = ((s0, s1, p0, p1), (s1, s0, p1, p0))

    def stages(first, count):
        for u in range(count):
            stage(first + u, *bufs[u % 2])

    def unrolled(t, carry):
        stages(ATTN_UNROLL * t, ATTN_UNROLL)
        return carry

    lax.fori_loop(0, i // ATTN_UNROLL, unrolled, 0)

    rem = i % ATTN_UNROLL
    for r in range(ATTN_UNROLL):
        @pl.when(rem == r)
        def _(r=r):
            stages(i - r, r)
            if r % 2 == 0:
                tail(s0, p0, p1)
            else:
                tail(s1, p1, p0)


def _attention(qvt, qf, k, kf, *, batch, seq, n_heads, tq):
    t = batch * seq
    nq = seq // tq
    return pl.pallas_call(
        functools.partial(_attn_kernel, tq=tq),
        grid=(batch, n_heads, nq),
        in_specs=[
            pl.BlockSpec((1, HEAD_DIM, tq), lambda b, h, i: (h, 0, b * nq + i)),
            pl.BlockSpec((1, F_ROWS, tq), lambda b, h, i: (h, 0, b * nq + i)),
            pl.BlockSpec((1, seq, HEAD_DIM), lambda b, h, i: (h, b, 0)),
            pl.BlockSpec((1, seq, V7X_LANES), lambda b, h, i: (h, b, 0)),
            pl.BlockSpec((1, HEAD_DIM, seq), lambda b, h, i: (n_heads + h, 0, b)),
        ],
        out_specs=pl.BlockSpec((tq, HEAD_DIM), lambda b, h, i: (b * nq + i, h)),
        out_shape=jax.ShapeDtypeStruct((t, n_heads * HEAD_DIM), F32),
        scratch_shapes=[pltpu.VMEM((2 * HEAD_DIM, tq), BF16), pltpu.VMEM((HEAD_DIM, tq), F32),
                        pltpu.VMEM((tq, tq), F32), pltpu.VMEM((tq, tq), F32),
                        pltpu.VMEM((tq, tq), BF16), pltpu.VMEM((tq, tq), BF16),
                        pltpu.VMEM((V7X_SUBLANES, tq), F32)],
        compiler_params=_params(("arbitrary", "arbitrary", "arbitrary")),
        name="forget_attention",
    )(qvt, qf, k, kf, qvt)


def _mixer_out_kernel(x_ref, ya_ref, cv_ref, scw_ref, cfw_ref, cfb_ref, lng_ref, lnb_ref, ong_ref,
                      wout_ref, g1_ref, n2g_ref, sh2_ref, sc2_ref, wr_hi_ref, wr_lo_ref,
                      xo_ref, hp_ref, lg_ref, zbuf, ubuf, conv_scr, *, nb, tm, sc_w, cf_w, attn_w):
    i = pl.program_id(0)

    @pl.when(i % nb == 0)
    def _():
        zbuf[0:SC_HALO, :] = jnp.zeros((SC_HALO, sc_w), F32)
        ubuf[0:CF_HALO, :] = jnp.zeros((CF_HALO, cf_w), F32)
        ubuf[CF_HALO + tm:, :] = jnp.zeros((V7X_SUBLANES, cf_w), F32)

    cv = cv_ref[...].astype(F32)
    sc_b = cv[:, 0:sc_w]
    sc_c = cv[:, sc_w:2 * sc_w]
    sc_u = cv[:, 2 * sc_w:3 * sc_w]
    cf_a = cv[:, 3 * sc_w:3 * sc_w + cf_w]
    cf_gate = cv[:, 3 * sc_w + cf_w:3 * sc_w + 2 * cf_w]

    zbuf[SC_HALO:SC_HALO + tm, :] = sc_c * sc_u
    conv = jnp.zeros((tm, sc_w), F32)
    for k in range(SC_KERNEL):
        off = SC_HALO - (SC_KERNEL - 1) + k
        conv = conv + scw_ref[k:k + 1, :] * zbuf[off:off + tm, :]
    y_sc = sc_b * conv
    zbuf[0:SC_HALO, :] = zbuf[tm:tm + SC_HALO, :]

    ubuf[CF_HALO:CF_HALO + tm, :] = cf_a * jax.nn.sigmoid(cf_gate)
    def conv_chunk(c, carry):
        r0 = pl.multiple_of((c // (cf_w // V7X_LANES)) * CONV_ROWS, CONV_ROWS)
        c0 = pl.multiple_of((c % (cf_w // V7X_LANES)) * V7X_LANES, V7X_LANES)
        cols = pl.ds(c0, V7X_LANES)
        acc = jnp.zeros((CONV_ROWS, V7X_LANES), F32) + cfb_ref[:, cols]
        for phase in range(V7X_SUBLANES):
            z = None
            for k in range(CF_KERNEL):
                off = CF_HALO - (CF_KERNEL - 1) + k
                if off % V7X_SUBLANES != phase:
                    continue
                win = pl.ds(pl.multiple_of(r0 + (off - phase), V7X_SUBLANES), CONV_ROWS + V7X_SUBLANES)
                term = cfw_ref[k:k + 1, cols] * ubuf[win, cols]
                z = term if z is None else z + term
            if z is not None:
                acc = acc + z[phase:phase + CONV_ROWS, :]
        conv_scr[pl.ds(r0, CONV_ROWS), cols] = acc
        return carry

    lax.fori_loop(0, (tm // CONV_ROWS) * (cf_w // V7X_LANES), conv_chunk, 0)
    ubuf[0:CF_HALO, :] = ubuf[tm:tm + CF_HALO, :]
    ong = ong_ref[...]
    w_hi = wr_hi_ref[...]
    rg = tm // MIX_ROW_GROUPS
    for grp in range(MIX_ROW_GROUPS):
        rows = slice(grp * rg, (grp + 1) * rg)
        conv = conv_scr[rows, :]
        mu = jnp.mean(conv, axis=-1, keepdims=True)
        cen = conv - mu
        var = jnp.mean(cen * cen, axis=-1, keepdims=True)
        y_cf = _silu(cen * lax.rsqrt(var + NORM_EPS) * lng_ref[...] + lnb_ref[...])
        y = jnp.concatenate([
            _rms(ya_ref[rows, :], ong[:, 0:attn_w]),
            _rms(y_sc[rows, :], ong[:, attn_w:attn_w + sc_w]),
            _rms(y_cf, ong[:, attn_w + sc_w:attn_w + sc_w + cf_w]),
        ], axis=-1).astype(BF16)
        mix = jnp.dot(y, wout_ref[...], preferred_element_type=F32)
        x_new = x_ref[rows, :] + g1_ref[0] * mix
        xo_ref[rows, :] = x_new

        h2 = _rms(x_new, n2g_ref[...]) * (1.0 + sc2_ref[0]) + sh2_ref[0]
        half = h2.shape[-1] // 2
        hp_ref[rows, :] = _pack_bf16_pair(h2[:, :half], h2[:, half:])
        h_hi = h2.astype(BF16)
        h_lo = (h2 - h_hi.astype(F32)).astype(BF16)
        lg_ref[:, rows] = _dot_nt(w_hi, h_hi) + _dot_nt(w_hi, h_lo) + _dot_nt(wr_lo_ref[...], h_hi)


def _mixer_out(x2, y_attn, conv_in, scw, cfw, cfb, lng, lnb, ong, w_out, mod, n2g, wr_hi, wr_lo,
               *, seq, tm):
    t, d = x2.shape
    attn_w = y_attn.shape[1]
    sc_w = scw.shape[1]
    cf_w = cfw.shape[1]
    conv_w = conv_in.shape[1]
    nb = seq // tm
    n_exp = wr_hi.shape[0]
    full = lambda a: pl.BlockSpec(a.shape, lambda i: (0,) * a.ndim)
    modspec = lambda c: pl.BlockSpec((1, 1, d), lambda i: (i // nb, 0, c))
    kern = functools.partial(_mixer_out_kernel, nb=nb, tm=tm, sc_w=sc_w, cf_w=cf_w, attn_w=attn_w)
    return pl.pallas_call(
        kern,
        grid=(t // tm,),
        in_specs=[
            pl.BlockSpec((tm, d), lambda i: (i, 0)),
            pl.BlockSpec((tm, attn_w), lambda i: (i, 0)),
            pl.BlockSpec((tm, conv_w), lambda i: (i, 0)),
            full(scw), full(cfw), full(cfb), full(lng), full(lnb), full(ong), full(w_out),
            modspec(2), full(n2g), modspec(3), modspec(4), full(wr_hi), full(wr_lo),
        ],
        out_specs=[
            pl.BlockSpec((tm, d), lambda i: (i, 0)),
            pl.BlockSpec((tm, d // 2), lambda i: (i, 0)),
            pl.BlockSpec((n_exp, tm), lambda i: (0, i)),
        ],
        out_shape=[
            jax.ShapeDtypeStruct((t, d), F32),
            jax.ShapeDtypeStruct((t, d // 2), U32),
            jax.ShapeDtypeStruct((n_exp, t), F32),
        ],
        scratch_shapes=[pltpu.VMEM((SC_HALO + tm, sc_w), F32),
                        pltpu.VMEM((CF_HALO + tm + V7X_SUBLANES, cf_w), F32),
                        pltpu.VMEM((tm, cf_w), F32)],
        compiler_params=_params(("arbitrary",)),
        name="conv_norm_outproj",
    )(x2, y_attn, conv_in, scw, cfw, cfb, lng, lnb, ong, w_out, mod, n2g, mod, mod, wr_hi, wr_lo)


def _col_max(x):
    return jnp.max(x, axis=0, keepdims=True)


def _col_min(x):
    return jnp.min(x, axis=0, keepdims=True)


def _router_kernel(lg_ref, bias_ref, eidx_ref, gate_ref, rank_ref, cnt_ref, carry, *, tr):
    @pl.when(pl.program_id(0) == 0)
    def _():
        carry[...] = jnp.zeros_like(carry)

    scores = jax.nn.sigmoid(lg_ref[...])
    biased = scores + bias_ref[...]
    rid = lax.broadcasted_iota(I32, (GROUP_SIZE, tr), 0)
    sc_g = [scores[GROUP_SIZE * g:GROUP_SIZE * (g + 1)] for g in range(N_GROUPS)]
    bi_g = [biased[GROUP_SIZE * g:GROUP_SIZE * (g + 1)] for g in range(N_GROUPS)]

    gs = []
    for g in range(N_GROUPS):
        m1 = _col_max(bi_g[g])
        first = _col_min(jnp.where(bi_g[g] == m1, rid, GROUP_SIZE))
        m2 = _col_max(jnp.where(rid == first, NEG_INF, bi_g[g]))
        gs.append(m1 + m2)
    grp = jnp.concatenate(gs, axis=0)

    keep = jnp.zeros((N_GROUPS, tr), F32)
    for _ in range(TOPK_GROUPS):
        m = _col_max(grp)
        first = _col_min(jnp.where(grp == m, rid, N_GROUPS))
        hit = rid == first
        keep = jnp.where(hit, 1.0, keep)
        grp = jnp.where(hit, NEG_INF, grp)

    cand = [jnp.where(keep[g:g + 1, :] > 0.5, bi_g[g], NEG_INF) for g in range(N_GROUPS)]
    eid_g = [rid + GROUP_SIZE * g for g in range(N_GROUPS)]
    onehot = [jnp.zeros((GROUP_SIZE, tr), F32) for _ in range(N_GROUPS)]
    e_sel, g_sel = [], []
    for _ in range(TOP_K):
        m = cand[0]
        for g in range(1, N_GROUPS):
            m = jnp.maximum(m, cand[g])
        m = _col_max(m)
        first = jnp.where(cand[0] == m, eid_g[0], N_EXPERTS)
        for g in range(1, N_GROUPS):
            first = jnp.minimum(first, jnp.where(cand[g] == m, eid_g[g], N_EXPERTS))
        first = _col_min(first)
        gate = jnp.zeros((GROUP_SIZE, tr), F32)
        for g in range(N_GROUPS):
            hit = eid_g[g] == first
            gate = gate + jnp.where(hit, sc_g[g], 0.0)
            cand[g] = jnp.where(hit, NEG_INF, cand[g])
            onehot[g] = jnp.where(hit, 1.0, onehot[g])
        e_sel.append(first)
        g_sel.append(jnp.sum(gate, axis=0, keepdims=True))

    gates = jnp.concatenate(g_sel, axis=0)
    gates = gates / (jnp.sum(gates, axis=0, keepdims=True) + 1e-20) * ROUTED_SCALE
    eidx = jnp.concatenate(e_sel, axis=0)
    eidx_ref[...] = eidx
    gate_ref[...] = gates

    oh = jnp.concatenate(onehot, axis=0)
    srow = lax.broadcasted_iota(I32, (tr, tr), 0)
    scol = lax.broadcasted_iota(I32, (tr, tr), 1)
    strict = (srow < scol).astype(BF16)
    before = jnp.dot(oh.astype(BF16), strict, preferred_element_type=F32) + carry[:, 0:1]
    ranks = []
    for k in range(TOP_K):
        acc = jnp.zeros((GROUP_SIZE, tr), F32)
        for g in range(N_GROUPS):
            acc = acc + jnp.where(eid_g[g] == e_sel[k], before[GROUP_SIZE * g:GROUP_SIZE * (g + 1)], 0.0)
        ranks.append(jnp.sum(acc, axis=0, keepdims=True))
    rank_ref[...] = jnp.concatenate(ranks, axis=0).astype(I32)
    carry[...] = carry[...] + jnp.sum(oh, axis=1, keepdims=True)
    cnt_ref[...] = carry[...].astype(I32)


def _router(logits_t, bias_b, *, tr):
    n_exp, t = logits_t.shape
    return pl.pallas_call(
        functools.partial(_router_kernel, tr=tr),
        grid=(t // tr,),
        in_specs=[
            pl.BlockSpec((n_exp, tr), lambda i: (0, i)),
            pl.BlockSpec((n_exp, tr), lambda i: (0, 0)),
        ],
        out_specs=[
            pl.BlockSpec((TOP_K, tr), lambda i: (0, i)),
            pl.BlockSpec((TOP_K, tr), lambda i: (0, i)),
            pl.BlockSpec((TOP_K, tr), lambda i: (0, i)),
            pl.BlockSpec((n_exp, V7X_LANES), lambda i: (0, 0)),
        ],
        out_shape=[
            jax.ShapeDtypeStruct((TOP_K, t), I32),
            jax.ShapeDtypeStruct((TOP_K, t), F32),
            jax.ShapeDtypeStruct((TOP_K, t), I32),
            jax.ShapeDtypeStruct((n_exp, V7X_LANES), I32),
        ],
        scratch_shapes=[pltpu.VMEM((n_exp, V7X_LANES), F32)],
        compiler_params=_params(("arbitrary",)),
        name="router_select",
    )(logits_t, bias_b)


def _dest_kernel(offs_ref, eidx_ref, rank_ref, dest_ref):
    eidx = eidx_ref[...]
    base = jnp.zeros(eidx.shape, I32)
    for e in range(N_EXPERTS):
        base = jnp.where(eidx == e, offs_ref[e], base)
    dest_ref[...] = base + rank_ref[...]


def _dest(offs, eidx_t, rank_t, *, tr):
    t = eidx_t.shape[1]
    return pl.pallas_call(
        _dest_kernel,
        grid_spec=pltpu.PrefetchScalarGridSpec(
            num_scalar_prefetch=1,
            grid=(t // tr,),
            in_specs=[pl.BlockSpec((TOP_K, tr), lambda i, o: (0, i)),
                      pl.BlockSpec((TOP_K, tr), lambda i, o: (0, i))],
            out_specs=pl.BlockSpec((TOP_K, tr), lambda i, o: (0, i)),
        ),
        out_shape=jax.ShapeDtypeStruct((TOP_K, t), I32),
        compiler_params=_params(("arbitrary",)),
        name="dispatch_rows",
    )(offs, eidx_t, rank_t)


def _expert_kernel(be_ref, used_ref, src0_ref, srcn_ref, dstp_ref, dstc_ref, h_ref, wg_ref, wu_ref, wd_ref,
                   out_ref, wg_s, wu_s, wd_s, xbuf, ybuf, gsem, ssem, *, spare_row):
    n = pl.program_id(0)
    used = used_ref[0]
    slot = n % 2
    other = 1 - slot

    def gather(src_ref, s):
        for r in range(EXPERT_BLOCK):
            pltpu.make_async_copy(h_ref.at[src_ref[0, 0, r]], xbuf.at[s, pl.ds(r, 1)],
                                  gsem.at[s]).start(priority=r % 2)

    def scatter(dst_ref, s):
        for r in range(EXPERT_BLOCK):
            pltpu.make_async_copy(ybuf.at[s, pl.ds(r, 1)], out_ref.at[pl.ds(dst_ref[0, 0, r], 1)],
                                  ssem.at[s]).start(priority=r % 2)

    def gather_wait(s):
        pltpu.make_async_copy(out_ref.at[pl.ds(0, EXPERT_BLOCK)], xbuf.at[s], gsem.at[s]).wait()

    def scatter_wait(s):
        pltpu.make_async_copy(ybuf.at[s], out_ref.at[pl.ds(0, EXPERT_BLOCK)], ssem.at[s]).wait()

    @pl.when(n == 0)
    def _():
        gather(src0_ref, 0)
        ybuf[...] = jnp.zeros_like(ybuf)
        pltpu.make_async_copy(ybuf.at[0], out_ref.at[pl.ds(spare_row, EXPERT_BLOCK)], ssem.at[0]).start()

    @pl.when((n < used) & ((n == 0) | (be_ref[n] != be_ref[jnp.maximum(n - 1, 0)])))
    def _():
        wg_s[...] = wg_ref[0].astype(BF16)
        wu_s[...] = wu_ref[0].astype(BF16)
        wd_s[...] = wd_ref[0].astype(BF16)

    def step(cur, oth):
        gather_wait(cur)
        scatter_wait(cur)
        gather(srcn_ref, oth)
        scatter(dstp_ref, oth)
        lo, hi = _unpack_bf16_pair(xbuf[cur])
        lo = lo.astype(BF16)
        hi = hi.astype(BF16)
        half = lo.shape[-1]
        g = (jnp.dot(lo, wg_s[:half, :], preferred_element_type=F32)
             + jnp.dot(hi, wg_s[half:, :], preferred_element_type=F32))
        u = (jnp.dot(lo, wu_s[:half, :], preferred_element_type=F32)
             + jnp.dot(hi, wu_s[half:, :], preferred_element_type=F32))
        hid = (_silu(g) * u).astype(BF16)
        y = jnp.dot(hid, wd_s[...], preferred_element_type=F32)
        ybuf[cur] = _pack_bf16_pair(y[:, :half], y[:, half:])

    @pl.when((n < used) & (slot == 0))
    def _():
        step(0, 1)

    @pl.when((n < used) & (slot == 1))
    def _():
        step(1, 0)

    @pl.when(n == used - 1)
    def _():
        scatter(dstc_ref, slot)
        scatter_wait(other)
        scatter_wait(slot)
        gather_wait(other)


def _experts(block_expert, used, row_src, row_dst, h2p, wg, wu, wd, *, layer, n_out_rows, spare_row):
    n_blocks = row_src.shape[0]
    half = h2p.shape[-1]
    _, _, d, hid = wg.shape
    last = lambda n, used: jnp.minimum(n, used[0] - 1)
    wsel = lambda n, be, used: (layer, be[last(n, used)], 0, 0)
    smem = lambda imap: pl.BlockSpec((1, 1, EXPERT_BLOCK), imap, memory_space=pltpu.SMEM)
    return pl.pallas_call(
        functools.partial(_expert_kernel, spare_row=spare_row),
        grid_spec=pltpu.PrefetchScalarGridSpec(
            num_scalar_prefetch=2,
            grid=(n_blocks,),
            in_specs=[
                smem(lambda n, be, used: (0, 0, 0)),
                smem(lambda n, be, used: (jnp.minimum(n + 1, used[0] - 1), 0, 0)),
                smem(lambda n, be, used: (last(n, used), 0, 0)),
                smem(lambda n, be, used: (last(n, used) + 1, 0, 0)),
                pl.BlockSpec(memory_space=pl.ANY),
                pl.BlockSpec((None, 1, d, hid), wsel),
                pl.BlockSpec((None, 1, d, hid), wsel),
                pl.BlockSpec((None, 1, hid, d), wsel),
            ],
            out_specs=pl.BlockSpec(memory_space=pl.ANY),
            scratch_shapes=[pltpu.VMEM((d, hid), BF16), pltpu.VMEM((d, hid), BF16),
                            pltpu.VMEM((hid, d), BF16),
                            pltpu.VMEM((2, EXPERT_BLOCK, half), U32), pltpu.VMEM((2, EXPERT_BLOCK, half), U32),
                            pltpu.SemaphoreType.DMA((2,)), pltpu.SemaphoreType.DMA((2,))],
        ),
        out_shape=jax.ShapeDtypeStruct((n_out_rows, half), U32),
        compiler_params=_params(("arbitrary",)),
        name="expert_ffn",
    )(block_expert, used, row_src, row_src, row_dst, row_dst, h2p, wg, wu, wd)


def _combine_kernel(x_ref, hp_ref, gate_ref, *rest, td, final_norm):
    y_refs = rest[:TOP_K]
    wsg_ref, wsu_ref, wsd_ref, g2_ref, fng_ref, o_ref = rest[TOP_K:]
    lo, hi = _unpack_bf16_pair(hp_ref[...])
    lo = lo.astype(BF16)
    hi = hi.astype(BF16)
    half = lo.shape[-1]
    g = (jnp.dot(lo, wsg_ref[:half, :], preferred_element_type=F32)
         + jnp.dot(hi, wsg_ref[half:, :], preferred_element_type=F32))
    u = (jnp.dot(lo, wsu_ref[:half, :], preferred_element_type=F32)
         + jnp.dot(hi, wsu_ref[half:, :], preferred_element_type=F32))
    hid = (_silu(g) * u).astype(BF16)
    shared = jnp.dot(hid, wsd_ref[...], preferred_element_type=F32)

    gate = gate_ref[...]
    r_lo = jnp.zeros((td, half), F32)
    r_hi = jnp.zeros((td, half), F32)
    for k in range(TOP_K):
        y_lo, y_hi = _unpack_bf16_pair(y_refs[k][...])
        w = gate[:, k:k + 1]
        r_lo = r_lo + w * y_lo
        r_hi = r_hi + w * y_hi
    routed = jnp.concatenate([r_lo, r_hi], axis=-1)
    out = x_ref[...] + g2_ref[0] * (routed + shared)
    if final_norm:
        out = _rms(out, fng_ref[...])
    o_ref[...] = out


def _combine(x2, h2p, gate_rows, y8, wsg, wsu, wsd, mod, fng, *, seq, td, final_norm):
    t, d = x2.shape
    half = d // 2
    nb = seq // td
    n_tb = t // td
    full = lambda a: pl.BlockSpec(a.shape, lambda i: (0,) * a.ndim)
    y_specs = [pl.BlockSpec((td, half), functools.partial(lambda i, k: (k * n_tb + i, 0), k=k))
               for k in range(TOP_K)]
    return pl.pallas_call(
        functools.partial(_combine_kernel, td=td, final_norm=final_norm),
        grid=(n_tb,),
        in_specs=[
            pl.BlockSpec((td, d), lambda i: (i, 0)),
            pl.BlockSpec((td, half), lambda i: (i, 0)),
            pl.BlockSpec((td, TOP_K), lambda i: (i, 0)),
            *y_specs,
            full(wsg), full(wsu), full(wsd),
            pl.BlockSpec((1, 1, d), lambda i: (i // nb, 0, 5)),
            full(fng),
        ],
        out_specs=pl.BlockSpec((td, d), lambda i: (i, 0)),
        out_shape=jax.ShapeDtypeStruct((t, d), F32),
        compiler_params=_params(("arbitrary",)),
        name="shared_combine",
    )(x2, h2p, gate_rows, *([y8] * TOP_K), wsg, wsu, wsd, mod, fng)


def kernel(x, c, w_ada, b_ada, norm1_g, norm2_g, w_in, b_forget, sc_conv_w, cf_conv_w, cf_conv_b,
           cf_ln_g, cf_ln_b, out_norm_g, w_out, w_router, router_bias, w_gate, w_up, w_down,
           ws_gate, ws_up, ws_down, final_norm_g):
    batch, seq, d = x.shape
    depth = w_ada.shape[0]
    t = batch * seq
    n_heads = b_forget.shape[1]
    attn_w = n_heads * HEAD_DIM
    sc_w = sc_conv_w.shape[2]
    cf_w = cf_conv_w.shape[2]
    conv_w = 3 * sc_w + 2 * cf_w
    assert w_in.shape[2] == 3 * attn_w + n_heads + conv_w
    assert batch <= V7X_SUBLANES and n_heads <= V7X_SUBLANES
    tl = _tiles(seq)
    n_assign = t * TOP_K
    n_blocks = -(-(n_assign + N_EXPERTS * (EXPERT_BLOCK - 1)) // EXPERT_BLOCK)
    n_rows = n_blocks * EXPERT_BLOCK

    c_pad = jnp.pad(c, ((0, V7X_SUBLANES - batch), (0, 0)))
    mod_all = _ada(c_pad, w_ada, b_ada)

    x2 = x.reshape(t, d)
    for l in range(depth):
        mod = mod_all[l, :batch].reshape(batch, 1, N_MOD * d)
        wl = w_in[l]
        w_t = jnp.concatenate([wl[:, :attn_w], wl[:, 2 * attn_w:3 * attn_w]], axis=1).T.astype(BF16)
        w_n = jnp.concatenate([wl[:, attn_w:2 * attn_w], wl[:, 3 * attn_w + n_heads:]], axis=1).astype(BF16)
        w_fcols = wl[:, 3 * attn_w:3 * attn_w + n_heads]
        w_f = jnp.pad(w_fcols, ((0, 0), (0, V7X_LANES - n_heads))).astype(BF16)
        w_ft = jnp.pad(w_fcols.T, ((0, 2 * V7X_SUBLANES - n_heads), (0, 0))).astype(BF16)
        qvt, k, conv_in, f_rows, f_t = _inproj(
            x2, norm1_g[l].reshape(1, d), mod, w_t, w_n, w_f, w_ft,
            seq=seq, attn_width=attn_w, conv_width=conv_w, tm=tl["tm_in"], tn=tl["tn_in"])

        bias_row = jnp.pad(b_forget[l], (0, V7X_LANES - n_heads)).reshape(1, V7X_LANES)
        bias_col = jnp.broadcast_to(
            jnp.pad(b_forget[l], (0, 2 * V7X_SUBLANES - n_heads))[:, None], (2 * V7X_SUBLANES, V7X_LANES))
        kf, qf = _fcum(f_rows, f_t, bias_row, bias_col, seq=seq, tc=tl["tc"], n_heads=n_heads)

        y_attn = _attention(qvt, qf, k, kf, batch=batch, seq=seq, n_heads=n_heads, tq=tl["tq"])

        scw = jnp.pad(sc_conv_w[l], ((0, V7X_SUBLANES - SC_KERNEL), (0, 0)))
        cfw = jnp.pad(cf_conv_w[l], ((0, CF_HALO - CF_KERNEL), (0, 0)))
        wr_t = w_router[l].T
        wr_hi = wr_t.astype(BF16)
        wr_lo = (wr_t - wr_hi.astype(F32)).astype(BF16)
        x2, h2p, logits_t = _mixer_out(
            x2, y_attn, conv_in, scw, cfw, cf_conv_b[l].reshape(1, cf_w), cf_ln_g[l].reshape(1, cf_w),
            cf_ln_b[l].reshape(1, cf_w), out_norm_g[l].reshape(1, d), w_out[l].astype(BF16), mod,
            norm2_g[l].reshape(1, d), wr_hi, wr_lo, seq=seq, tm=tl["tm_mix"])

        bias_b = jnp.broadcast_to(router_bias[l][:, None], (N_EXPERTS, tl["tr"]))
        eidx_t, gate_t, rank_t, counts = _router(logits_t, bias_b, tr=tl["tr"])
        counts = counts[:, 0]
        padded = (counts + EXPERT_BLOCK - 1) // EXPERT_BLOCK * EXPERT_BLOCK
        pad_end = jnp.cumsum(padded).astype(I32)
        offs = pad_end - padded
        dest_t = _dest(offs, eidx_t, rank_t, tr=tl["tr"])
        block_start = jnp.arange(n_blocks, dtype=I32) * EXPERT_BLOCK
        block_expert = jnp.minimum(
            jnp.sum((pad_end[None, :] <= block_start[:, None]).astype(I32), axis=1), N_EXPERTS - 1)
        used = (pad_end[-1:] // EXPERT_BLOCK).astype(I32)

        row_assign = jnp.full((n_rows,), -1, I32).at[dest_t.reshape(-1)].set(
            jnp.arange(n_assign, dtype=I32), unique_indices=True)
        is_real = row_assign >= 0
        row_src = jnp.where(is_real, row_assign % t, 0).reshape(n_blocks, 1, EXPERT_BLOCK)
        row_dst = jnp.where(is_real, row_assign, n_assign + jnp.arange(n_rows, dtype=I32))
        spare_block = n_assign + n_rows + jnp.arange(EXPERT_BLOCK, dtype=I32)
        row_dst = jnp.concatenate([spare_block, row_dst]).reshape(n_blocks + 1, 1, EXPERT_BLOCK)

        y8 = _experts(block_expert, used, row_src, row_dst, h2p.reshape(t, 1, d // 2), w_gate, w_up, w_down,
                      layer=l,
                      n_out_rows=n_assign + n_rows + 2 * EXPERT_BLOCK,
                      spare_row=n_assign + n_rows + EXPERT_BLOCK)
        x2 = _combine(x2, h2p, gate_t.T, y8, ws_gate[l].astype(BF16), ws_up[l].astype(BF16),
                      ws_down[l].astype(BF16), mod, final_norm_g.reshape(1, d),
                      seq=seq, td=tl["td"], final_norm=(l == depth - 1))
    return x2.reshape(batch, seq, d)
```

```python
import functools
import math

import jax
import jax.numpy as jnp
from jax import lax
from jax.experimental import pallas as pl
from jax.experimental.pallas import tpu as pltpu

F32 = jnp.float32
BF16 = jnp.bfloat16
I32 = jnp.int32
U32 = jnp.uint32

V7X_LANES = 128
V7X_SUBLANES = 8
V7X_VMEM_BYTES = 64 * 1024 * 1024
VMEM_LIMIT = V7X_VMEM_BYTES - 8 * 1024 * 1024

HEAD_DIM = 128
N_EXPERTS = 64
TOP_K = 8
N_GROUPS = 8
GROUP_SIZE = N_EXPERTS // N_GROUPS
TOPK_GROUPS = 4
ROUTED_SCALE = 2.5
EXPERT_BLOCK = 256
SC_KERNEL = 3
CF_KERNEL = 31
N_MOD = 6
NORM_EPS = 1e-6
SC_HALO = 8
CF_HALO = 32
NEG_INF = float("-inf")
LOG2E = math.log2(math.e)
F_ROWS = 16
ATTN_UNROLL = 4
CONV_ROWS = 64
MIX_ROW_GROUPS = 2


def _params(semantics, **kw):
    return pltpu.CompilerParams(dimension_semantics=semantics, vmem_limit_bytes=VMEM_LIMIT, **kw)


def _tiles(seq):
    return dict(
        tm_in=min(1024, seq),
        tn_in=512,
        tc=min(512, seq),
        tq=min(512, seq),
        tm_mix=min(256, seq),
        tr=min(512, seq),
        td=min(256, seq),
    )


def _rms(x, g):
    return x * lax.rsqrt(jnp.mean(x * x, axis=-1, keepdims=True) + NORM_EPS) * g


def _silu(x):
    return x * jax.nn.sigmoid(x)


def _pack_bf16_pair(lo, hi):
    lo_b = lax.bitcast_convert_type(lo.astype(BF16).astype(F32), U32) >> 16
    hi_b = lax.bitcast_convert_type(hi.astype(BF16).astype(F32), U32) & jnp.uint32(0xFFFF0000)
    return hi_b | lo_b


def _unpack_bf16_pair(p):
    lo = lax.bitcast_convert_type(p << 16, F32)
    hi = lax.bitcast_convert_type(p & jnp.uint32(0xFFFF0000), F32)
    return lo, hi


def _split3(x):
    a = x.astype(BF16)
    r = x - a.astype(F32)
    b = r.astype(BF16)
    c = (r - b.astype(F32)).astype(BF16)
    return a, b, c


def _dot_nt(a, b):
    return lax.dot_general(a, b, (((1,), (1,)), ((), ())), preferred_element_type=F32)


def _ada_kernel(c_ref, w_ref, b_ref, o_ref):
    c = c_ref[...]
    ca = _silu(c).astype(BF16)
    o_ref[0] = jnp.dot(ca, w_ref[0].astype(BF16), preferred_element_type=F32) + b_ref[0]


def _ada(c_pad, w_ada, b_ada):
    depth, d, n = w_ada.shape
    tn = 1024
    return pl.pallas_call(
        _ada_kernel,
        grid=(depth, n // tn),
        in_specs=[
            pl.BlockSpec((V7X_SUBLANES, d), lambda l, j: (0, 0)),
            pl.BlockSpec((1, d, tn), lambda l, j: (l, 0, j)),
            pl.BlockSpec((1, 1, tn), lambda l, j: (l, 0, j)),
        ],
        out_specs=pl.BlockSpec((1, V7X_SUBLANES, tn), lambda l, j: (l, 0, j)),
        out_shape=jax.ShapeDtypeStruct((depth, V7X_SUBLANES, n), F32),
        compiler_params=_params(("arbitrary", "arbitrary")),
        name="ada_mod",
    )(c_pad, w_ada, b_ada.reshape(depth, 1, n))


def _inproj_kernel(x_ref, g_ref, sh_ref, sc_ref, wt_ref, w_ref, wf_ref, wft_ref,
                   qvt_ref, k_ref, conv_ref, fr_ref, ft_ref, h_scr,
                   *, n_q_steps, n_t_steps, n_k_steps, heads_per_step):
    j = pl.program_id(1)

    @pl.when(j == 0)
    def _():
        h = _rms(x_ref[...], g_ref[...]) * (1.0 + sc_ref[0]) + sh_ref[0]
        hb = h.astype(BF16)
        h_scr[...] = hb
        fr_ref[...] = jnp.dot(hb, wf_ref[...], preferred_element_type=F32)
        ft_ref[...] = _dot_nt(wft_ref[...], hb)

    @pl.when(j < n_t_steps)
    def _():
        scale = jnp.where(j < n_q_steps, LOG2E * HEAD_DIM ** -0.5, 1.0).astype(F32)
        a = _dot_nt(wt_ref[...], h_scr[...]) * scale
        for u in range(heads_per_step):
            qvt_ref[u] = a[u * HEAD_DIM:(u + 1) * HEAD_DIM, :].astype(BF16)

    @pl.when(j >= n_t_steps)
    def _():
        a = jnp.dot(h_scr[...], w_ref[...], preferred_element_type=F32)

        @pl.when(j < n_t_steps + n_k_steps)
        def _():
            for u in range(heads_per_step):
                k_ref[u] = a[:, u * HEAD_DIM:(u + 1) * HEAD_DIM].astype(BF16)

        @pl.when(j >= n_t_steps + n_k_steps)
        def _():
            conv_ref[...] = a.astype(BF16)


def _inproj(x2, g1, mod, w_t, w_n, w_f, w_ft, *, seq, attn_width, conv_width, tm, tn):
    t, d = x2.shape
    nb = seq // tm
    n_q_steps = attn_width // tn
    n_t_steps = 2 * attn_width // tn
    n_k_steps = attn_width // tn
    n_c_steps = conv_width // tn
    n_steps = n_t_steps + n_k_steps + n_c_steps
    hps = tn // HEAD_DIM
    n_heads = attn_width // HEAD_DIM
    kern = functools.partial(_inproj_kernel, n_q_steps=n_q_steps, n_t_steps=n_t_steps,
                             n_k_steps=n_k_steps, heads_per_step=hps)
    return pl.pallas_call(
        kern,
        grid=(t // tm, n_steps),
        in_specs=[
            pl.BlockSpec((tm, d), lambda i, j: (i, 0)),
            pl.BlockSpec((1, d), lambda i, j: (0, 0)),
            pl.BlockSpec((1, 1, d), lambda i, j: (i // nb, 0, 0)),
            pl.BlockSpec((1, 1, d), lambda i, j: (i // nb, 0, 1)),
            pl.BlockSpec((tn, d), lambda i, j: (jnp.minimum(j, n_t_steps - 1), 0)),
            pl.BlockSpec((d, tn), lambda i, j: (0, jnp.maximum(j - n_t_steps, 0))),
            pl.BlockSpec((d, V7X_LANES), lambda i, j: (0, 0)),
            pl.BlockSpec((2 * V7X_SUBLANES, d), lambda i, j: (0, 0)),
        ],
        out_specs=[
            pl.BlockSpec((hps, HEAD_DIM, tm), lambda i, j: (jnp.minimum(j, n_t_steps - 1), 0, i)),
            pl.BlockSpec((hps, tm, HEAD_DIM),
                         lambda i, j: (jnp.clip(j - n_t_steps, 0, n_k_steps - 1), i, 0)),
            pl.BlockSpec((tm, tn),
                         lambda i, j: (i, jnp.clip(j - n_t_steps - n_k_steps, 0, n_c_steps - 1))),
            pl.BlockSpec((tm, V7X_LANES), lambda i, j: (i, 0)),
            pl.BlockSpec((2 * V7X_SUBLANES, tm), lambda i, j: (0, i)),
        ],
        out_shape=[
            jax.ShapeDtypeStruct((2 * n_heads, HEAD_DIM, t), BF16),
            jax.ShapeDtypeStruct((n_heads, t, HEAD_DIM), BF16),
            jax.ShapeDtypeStruct((t, conv_width), BF16),
            jax.ShapeDtypeStruct((t, V7X_LANES), F32),
            jax.ShapeDtypeStruct((2 * V7X_SUBLANES, t), F32),
        ],
        scratch_shapes=[pltpu.VMEM((tm, d), BF16)],
        compiler_params=_params(("arbitrary", "arbitrary")),
        name="norm_inproj",
    )(x2, g1, mod, mod, w_t, w_n, w_f, w_ft)


def _log_sigmoid(z):
    return jnp.minimum(z, 0.0) - jnp.log1p(jnp.exp(-jnp.abs(z)))


def _fcum_kernel(fr_ref, ft_ref, br_ref, bt_ref, kf_ref, qf_ref, car_r, car_t, *, nb, tc, n_heads):
    i = pl.program_id(0)

    @pl.when(i % nb == 0)
    def _():
        car_r[...] = jnp.zeros_like(car_r)
        car_t[...] = jnp.zeros_like(car_t)

    row = lax.broadcasted_iota(I32, (tc, tc), 0)
    col = lax.broadcasted_iota(I32, (tc, tc), 1)
    lower = (row >= col).astype(BF16)
    upper = (row <= col).astype(BF16)

    lf_r = _log_sigmoid(fr_ref[...] + br_ref[...])
    cum_r = car_r[0:1, :]
    for part in _split3(lf_r):
        cum_r = cum_r + jnp.dot(lower, part, preferred_element_type=F32)
    car_r[...] = jnp.broadcast_to(cum_r[tc - 1:tc, :], car_r.shape)

    lf_t = _log_sigmoid(ft_ref[...] + bt_ref[:, 0:1])
    cum_t = car_t[:, 0:1]
    for part in _split3(lf_t):
        cum_t = cum_t + jnp.dot(part, upper, preferred_element_type=F32)
    car_t[...] = jnp.broadcast_to(cum_t[:, tc - 1:tc], car_t.shape)

    lane = lax.broadcasted_iota(I32, (tc, V7X_LANES), 1)
    sub = lax.broadcasted_iota(I32, (F_ROWS, tc), 0)
    for h in range(n_heads):
        k_hi, k_mid, k_lo = _split3(jnp.broadcast_to(cum_r[:, h:h + 1] * LOG2E, (tc, V7X_LANES)))
        ones = jnp.where(lane < 6, 1.0, 0.0).astype(BF16)
        kf_ref[h] = jnp.where(lane == 0, k_hi, jnp.where(lane == 1, k_mid, jnp.where(lane == 2, k_lo, ones)))
        q_hi, q_mid, q_lo = _split3(jnp.broadcast_to(cum_t[h:h + 1, :] * LOG2E, (F_ROWS, tc)))
        neg = jnp.where(sub < 3, -1.0, 0.0).astype(BF16)
        qf_ref[h] = jnp.where(sub == 3, q_hi, jnp.where(sub == 4, q_mid, jnp.where(sub == 5, q_lo, neg)))


def _fcum(f_rows, f_t, bias_row, bias_col, *, seq, tc, n_heads):
    t = f_rows.shape[0]
    nb = seq // tc
    rows_t = f_t.shape[0]
    return pl.pallas_call(
        functools.partial(_fcum_kernel, nb=nb, tc=tc, n_heads=n_heads),
        grid=(t // tc,),
        in_specs=[
            pl.BlockSpec((tc, V7X_LANES), lambda i: (i, 0)),
            pl.BlockSpec((rows_t, tc), lambda i: (0, i)),
            pl.BlockSpec((1, V7X_LANES), lambda i: (0, 0)),
            pl.BlockSpec((rows_t, V7X_LANES), lambda i: (0, 0)),
        ],
        out_specs=[
            pl.BlockSpec((n_heads, tc, V7X_LANES), lambda i: (0, i, 0)),
            pl.BlockSpec((n_heads, F_ROWS, tc), lambda i: (0, 0, i)),
        ],
        out_shape=[
            jax.ShapeDtypeStruct((n_heads, t, V7X_LANES), BF16),
            jax.ShapeDtypeStruct((n_heads, F_ROWS, t), BF16),
        ],
        scratch_shapes=[pltpu.VMEM((V7X_SUBLANES, V7X_LANES), F32),
                        pltpu.VMEM((rows_t, V7X_LANES), F32)],
        compiler_params=_params(("arbitrary",)),
        name="forget_cumsum",
    )(f_rows, f_t, bias_row, bias_col)


def _attn_kernel(q_ref, qf_ref, k_ref, kf_ref, vt_ref, o_ref, qa_scr, acc_scr, s0, s1, p0, p1, st_scr,
                 *, tq):
    i = pl.program_id(2)
    qa_scr[0:HEAD_DIM, :] = q_ref[0]
    qa_scr[HEAD_DIM:HEAD_DIM + F_ROWS, :] = qf_ref[0]
    qa_scr[HEAD_DIM + F_ROWS:, :] = jnp.zeros((HEAD_DIM - F_ROWS, tq), BF16)
    acc_scr[...] = jnp.zeros_like(acc_scr)
    p1[...] = jnp.zeros_like(p1)
    st_scr[0:1, :] = jnp.full((1, tq), NEG_INF, F32)
    st_scr[1:2, :] = jnp.zeros((1, tq), F32)
    st_scr[2:3, :] = jnp.ones((1, tq), F32)

    def logits(j, s_out):
        start = pl.multiple_of(j * tq, tq)
        kk = jnp.concatenate([k_ref[0, pl.ds(start, tq), :], kf_ref[0, pl.ds(start, tq), :]], axis=1)
        s_out[...] = jnp.dot(kk, qa_scr[...], preferred_element_type=F32)

    def values(j, p_in):
        start = pl.multiple_of(jnp.maximum(j, 0) * tq, tq)
        vt = vt_ref[0, :, pl.ds(start, tq)]
        acc_scr[...] = st_scr[2:3, :] * acc_scr[...] + jnp.dot(vt, p_in[...], preferred_element_type=F32)

    def softmax(s_in, p_out, masked):
        s = s_in[...]
        if masked:
            key = lax.broadcasted_iota(I32, (tq, tq), 0)
            qry = lax.broadcasted_iota(I32, (tq, tq), 1)
            s = jnp.where(key <= qry, s, NEG_INF)
        m_prev = st_scr[0:1, :]
        m_new = jnp.maximum(m_prev, jnp.max(s, axis=0, keepdims=True))
        alpha = jnp.exp2(m_prev - m_new)
        p = jnp.exp2(s - m_new)
        st_scr[1:2, :] = alpha * st_scr[1:2, :] + jnp.sum(p, axis=0, keepdims=True)
        st_scr[0:1, :] = m_new
        st_scr[2:3, :] = alpha
        p_out[...] = p.astype(BF16)

    def stage(j, s_cur, s_nxt, p_cur, p_prev):
        values(j - 1, p_prev)
        softmax(s_cur, p_cur, False)
        logits(j + 1, s_nxt)

    def tail(s_cur, p_cur, p_prev):
        values(i - 1, p_prev)
        softmax(s_cur, p_cur, True)
        values(i, p_cur)
        o_ref[...] = (acc_scr[...] / st_scr[1:2, :]).T

    logits(0, s0)

    bufs = ((s0, s1, p0, p1), (s1, s0, p1, p0))

    def stages(first, count):
        for u in range(count):
            stage(first + u, *bufs[u % 2])

    def unrolled(t, carry):
        stages(ATTN_UNROLL * t, ATTN_UNROLL)
        return carry

    lax.fori_loop(0, i // ATTN_UNROLL, unrolled, 0)

    rem = i % ATTN_UNROLL
    for r in range(ATTN_UNROLL):
        @pl.when(rem == r)
        def _(r=r):
            stages(i - r, r)
            if r % 2 == 0:
                tail(s0, p0, p1)
            else:
                tail(s1, p1, p0)


def _attention(qvt, qf, k, kf, *, batch, seq, n_heads, tq):
    t = batch * seq
    nq = seq // tq
    return pl.pallas_call(
        functools.partial(_attn_kernel, tq=tq),
        grid=(batch, n_heads, nq),
        in_specs=[
            pl.BlockSpec((1, HEAD_DIM, tq), lambda b, h, i: (h, 0, b * nq + i)),
            pl.BlockSpec((1, F_ROWS, tq), lambda b, h, i: (h, 0, b * nq + i)),
            pl.BlockSpec((1, seq, HEAD_DIM), lambda b, h, i: (h, b, 0)),
            pl.BlockSpec((1, seq, V7X_LANES), lambda b, h, i: (h, b, 0)),
            pl.BlockSpec((1, HEAD_DIM, seq), lambda b, h, i: (n_heads + h, 0, b)),
        ],
        out_specs=pl.BlockSpec((tq, HEAD_DIM), lambda b, h, i: (b * nq + i, h)),
        out_shape=jax.ShapeDtypeStruct((t, n_heads * HEAD_DIM), F32),
        scratch_shapes=[pltpu.VMEM((2 * HEAD_DIM, tq), BF16), pltpu.VMEM((HEAD_DIM, tq), F32),
                        pltpu.VMEM((tq, tq), F32), pltpu.VMEM((tq, tq), F32),
                        pltpu.VMEM((tq, tq), BF16), pltpu.VMEM((tq, tq), BF16),
                        pltpu.VMEM((V7X_SUBLANES, tq), F32)],
        compiler_params=_params(("arbitrary", "arbitrary", "arbitrary")),
        name="forget_attention",
    )(qvt, qf, k, kf, qvt)


def _mixer_out_kernel(x_ref, ya_ref, cv_ref, scw_ref, cfw_ref, cfb_ref, lng_ref, lnb_ref, ong_ref,
                      wout_ref, g1_ref, n2g_ref, sh2_ref, sc2_ref, wr_hi_ref, wr_lo_ref,
                      xo_ref, hp_ref, lg_ref, zbuf, ubuf, conv_scr, *, nb, tm, sc_w, cf_w, attn_w):
    i = pl.program_id(0)

    @pl.when(i % nb == 0)
    def _():
        zbuf[0:SC_HALO, :] = jnp.zeros((SC_HALO, sc_w), F32)
        ubuf[0:CF_HALO, :] = jnp.zeros((CF_HALO, cf_w), F32)
        ubuf[CF_HALO + tm:, :] = jnp.zeros((V7X_SUBLANES, cf_w), F32)

    cv = cv_ref[...].astype(F32)
    sc_b = cv[:, 0:sc_w]
    sc_c = cv[:, sc_w:2 * sc_w]
    sc_u = cv[:, 2 * sc_w:3 * sc_w]
    cf_a = cv[:, 3 * sc_w:3 * sc_w + cf_w]
    cf_gate = cv[:, 3 * sc_w + cf_w:3 * sc_w + 2 * cf_w]

    zbuf[SC_HALO:SC_HALO + tm, :] = sc_c * sc_u
    conv = jnp.zeros((tm, sc_w), F32)
    for k in range(SC_KERNEL):
        off = SC_HALO - (SC_KERNEL - 1) + k
        conv = conv + scw_ref[k:k + 1, :] * zbuf[off:off + tm, :]
    y_sc = sc_b * conv
    zbuf[0:SC_HALO, :] = zbuf[tm:tm + SC_HALO, :]

    ubuf[CF_HALO:CF_HALO + tm, :] = cf_a * jax.nn.sigmoid(cf_gate)
    def conv_chunk(c, carry):
        r0 = pl.multiple_of((c // (cf_w // V7X_LANES)) * CONV_ROWS, CONV_ROWS)
        c0 = pl.multiple_of((c % (cf_w // V7X_LANES)) * V7X_LANES, V7X_LANES)
        cols = pl.ds(c0, V7X_LANES)
        acc = jnp.zeros((CONV_ROWS, V7X_LANES), F32) + cfb_ref[:, cols]
        for phase in range(V7X_SUBLANES):
            z = None
            for k in range(CF_KERNEL):
                off = CF_HALO - (CF_KERNEL - 1) + k
                if off % V7X_SUBLANES != phase:
                    continue
                win = pl.ds(pl.multiple_of(r0 + (off - phase), V7X_SUBLANES), CONV_ROWS + V7X_SUBLANES)
                term = cfw_ref[k:k + 1, cols] * ubuf[win, cols]
                z = term if z is None else z + term
            if z is not None:
                acc = acc + z[phase:phase + CONV_ROWS, :]
        conv_scr[pl.ds(r0, CONV_ROWS), cols] = acc
        return carry

    lax.fori_loop(0, (tm // CONV_ROWS) * (cf_w // V7X_LANES), conv_chunk, 0)
    ubuf[0:CF_HALO, :] = ubuf[tm:tm + CF_HALO, :]
    ong = ong_ref[...]
    w_hi = wr_hi_ref[...]
    rg = tm // MIX_ROW_GROUPS
    for grp in range(MIX_ROW_GROUPS):
        rows = slice(grp * rg, (grp + 1) * rg)
        conv = conv_scr[rows, :]
        mu = jnp.mean(conv, axis=-1, keepdims=True)
        cen = conv - mu
        var = jnp.mean(cen * cen, axis=-1, keepdims=True)
        y_cf = _silu(cen * lax.rsqrt(var + NORM_EPS) * lng_ref[...] + lnb_ref[...])
        y = jnp.concatenate([
            _rms(ya_ref[rows, :], ong[:, 0:attn_w]),
            _rms(y_sc[rows, :], ong[:, attn_w:attn_w + sc_w]),
            _rms(y_cf, ong[:, attn_w + sc_w:attn_w + sc_w + cf_w]),
        ], axis=-1).astype(BF16)
        mix = jnp.dot(y, wout_ref[...], preferred_element_type=F32)
        x_new = x_ref[rows, :] + g1_ref[0] * mix
        xo_ref[rows, :] = x_new

        h2 = _rms(x_new, n2g_ref[...]) * (1.0 + sc2_ref[0]) + sh2_ref[0]
        half = h2.shape[-1] // 2
        hp_ref[rows, :] = _pack_bf16_pair(h2[:, :half], h2[:, half:])
        h_hi = h2.astype(BF16)
        h_lo = (h2 - h_hi.astype(F32)).astype(BF16)
        lg_ref[:, rows] = _dot_nt(w_hi, h_hi) + _dot_nt(w_hi, h_lo) + _dot_nt(wr_lo_ref[...], h_hi)


def _mixer_out(x2, y_attn, conv_in, scw, cfw, cfb, lng, lnb, ong, w_out, mod, n2g, wr_hi, wr_lo,
               *, seq, tm):
    t, d = x2.shape
    attn_w = y_attn.shape[1]
    sc_w = scw.shape[1]
    cf_w = cfw.shape[1]
    conv_w = conv_in.shape[1]
    nb = seq // tm
    n_exp = wr_hi.shape[0]
    full = lambda a: pl.BlockSpec(a.shape, lambda i: (0,) * a.ndim)
    modspec = lambda c: pl.BlockSpec((1, 1, d), lambda i: (i // nb, 0, c))
    kern = functools.partial(_mixer_out_kernel, nb=nb, tm=tm, sc_w=sc_w, cf_w=cf_w, attn_w=attn_w)
    return pl.pallas_call(
        kern,
        grid=(t // tm,),
        in_specs=[
            pl.BlockSpec((tm, d), lambda i: (i, 0)),
            pl.BlockSpec((tm, attn_w), lambda i: (i, 0)),
            pl.BlockSpec((tm, conv_w), lambda i: (i, 0)),
            full(scw), full(cfw), full(cfb), full(lng), full(lnb), full(ong), full(w_out),
            modspec(2), full(n2g), modspec(3), modspec(4), full(wr_hi), full(wr_lo),
        ],
        out_specs=[
            pl.BlockSpec((tm, d), lambda i: (i, 0)),
            pl.BlockSpec((tm, d // 2), lambda i: (i, 0)),
            pl.BlockSpec((n_exp, tm), lambda i: (0, i)),
        ],
        out_shape=[
            jax.ShapeDtypeStruct((t, d), F32),
            jax.ShapeDtypeStruct((t, d // 2), U32),
            jax.ShapeDtypeStruct((n_exp, t), F32),
        ],
        scratch_shapes=[pltpu.VMEM((SC_HALO + tm, sc_w), F32),
                        pltpu.VMEM((CF_HALO + tm + V7X_SUBLANES, cf_w), F32),
                        pltpu.VMEM((tm, cf_w), F32)],
        compiler_params=_params(("arbitrary",)),
        name="conv_norm_outproj",
    )(x2, y_attn, conv_in, scw, cfw, cfb, lng, lnb, ong, w_out, mod, n2g, mod, mod, wr_hi, wr_lo)


def _col_max(x):
    return jnp.max(x, axis=0, keepdims=True)


def _col_min(x):
    return jnp.min(x, axis=0, keepdims=True)


def _router_kernel(lg_ref, bias_ref, eidx_ref, gate_ref, rank_ref, cnt_ref, carry, *, tr):
    @pl.when(pl.program_id(0) == 0)
    def _():
        carry[...] = jnp.zeros_like(carry)

    scores = jax.nn.sigmoid(lg_ref[...])
    biased = scores + bias_ref[...]
    rid = lax.broadcasted_iota(I32, (GROUP_SIZE, tr), 0)
    sc_g = [scores[GROUP_SIZE * g:GROUP_SIZE * (g + 1)] for g in range(N_GROUPS)]
    bi_g = [biased[GROUP_SIZE * g:GROUP_SIZE * (g + 1)] for g in range(N_GROUPS)]

    gs = []
    for g in range(N_GROUPS):
        m1 = _col_max(bi_g[g])
        first = _col_min(jnp.where(bi_g[g] == m1, rid, GROUP_SIZE))
        m2 = _col_max(jnp.where(rid == first, NEG_INF, bi_g[g]))
        gs.append(m1 + m2)
    grp = jnp.concatenate(gs, axis=0)

    keep = jnp.zeros((N_GROUPS, tr), F32)
    for _ in range(TOPK_GROUPS):
        m = _col_max(grp)
        first = _col_min(jnp.where(grp == m, rid, N_GROUPS))
        hit = rid == first
        keep = jnp.where(hit, 1.0, keep)
        grp = jnp.where(hit, NEG_INF, grp)

    cand = [jnp.where(keep[g:g + 1, :] > 0.5, bi_g[g], NEG_INF) for g in range(N_GROUPS)]
    eid_g = [rid + GROUP_SIZE * g for g in range(N_GROUPS)]
    onehot = [jnp.zeros((GROUP_SIZE, tr), F32) for _ in range(N_GROUPS)]
    e_sel, g_sel = [], []
    for _ in range(TOP_K):
        m = cand[0]
        for g in range(1, N_GROUPS):
            m = jnp.maximum(m, cand[g])
        m = _col_max(m)
        first = jnp.where(cand[0] == m, eid_g[0], N_EXPERTS)
        for g in range(1, N_GROUPS):
            first = jnp.minimum(first, jnp.where(cand[g] == m, eid_g[g], N_EXPERTS))
        first = _col_min(first)
        gate = jnp.zeros((GROUP_SIZE, tr), F32)
        for g in range(N_GROUPS):
            hit = eid_g[g] == first
            gate = gate + jnp.where(hit, sc_g[g], 0.0)
            cand[g] = jnp.where(hit, NEG_INF, cand[g])
            onehot[g] = jnp.where(hit, 1.0, onehot[g])
        e_sel.append(first)
        g_sel.append(jnp.sum(gate, axis=0, keepdims=True))

    gates = jnp.concatenate(g_sel, axis=0)
    gates = gates / (jnp.sum(gates, axis=0, keepdims=True) + 1e-20) * ROUTED_SCALE
    eidx = jnp.concatenate(e_sel, axis=0)
    eidx_ref[...] = eidx
    gate_ref[...] = gates

    oh = jnp.concatenate(onehot, axis=0)
    srow = lax.broadcasted_iota(I32, (tr, tr), 0)
    scol = lax.broadcasted_iota(I32, (tr, tr), 1)
    strict = (srow < scol).astype(BF16)
    before = jnp.dot(oh.astype(BF16), strict, preferred_element_type=F32) + carry[:, 0:1]
    ranks = []
    for k in range(TOP_K):
        acc = jnp.zeros((GROUP_SIZE, tr), F32)
        for g in range(N_GROUPS):
            acc = acc + jnp.where(eid_g[g] == e_sel[k], before[GROUP_SIZE * g:GROUP_SIZE * (g + 1)], 0.0)
        ranks.append(jnp.sum(acc, axis=0, keepdims=True))
    rank_ref[...] = jnp.concatenate(ranks, axis=0).astype(I32)
    carry[...] = carry[...] + jnp.sum(oh, axis=1, keepdims=True)
    cnt_ref[...] = carry[...].astype(I32)


def _router(logits_t, bias_b, *, tr):
    n_exp, t = logits_t.shape
    return pl.pallas_call(
        functools.partial(_router_kernel, tr=tr),
        grid=(t // tr,),
        in_specs=[
            pl.BlockSpec((n_exp, tr), lambda i: (0, i)),
            pl.BlockSpec((n_exp, tr), lambda i: (0, 0)),
        ],
        out_specs=[
            pl.BlockSpec((TOP_K, tr), lambda i: (0, i)),
            pl.BlockSpec((TOP_K, tr), lambda i: (0, i)),
            pl.BlockSpec((TOP_K, tr), lambda i: (0, i)),
            pl.BlockSpec((n_exp, V7X_LANES), lambda i: (0, 0)),
        ],
        out_shape=[
            jax.ShapeDtypeStruct((TOP_K, t), I32),
            jax.ShapeDtypeStruct((TOP_K, t), F32),
            jax.ShapeDtypeStruct((TOP_K, t), I32),
            jax.ShapeDtypeStruct((n_exp, V7X_LANES), I32),
        ],
        scratch_shapes=[pltpu.VMEM((n_exp, V7X_LANES), F32)],
        compiler_params=_params(("arbitrary",)),
        name="router_select",
    )(logits_t, bias_b)


def _dest_kernel(offs_ref, eidx_ref, rank_ref, dest_ref):
    eidx = eidx_ref[...]
    base = jnp.zeros(eidx.shape, I32)
    for e in range(N_EXPERTS):
        base = jnp.where(eidx == e, offs_ref[e], base)
    dest_ref[...] = base + rank_ref[...]


def _dest(offs, eidx_t, rank_t, *, tr):
    t = eidx_t.shape[1]
    return pl.pallas_call(
        _dest_kernel,
        grid_spec=pltpu.PrefetchScalarGridSpec(
            num_scalar_prefetch=1,
            grid=(t // tr,),
            in_specs=[pl.BlockSpec((TOP_K, tr), lambda i, o: (0, i)),
                      pl.BlockSpec((TOP_K, tr), lambda i, o: (0, i))],
            out_specs=pl.BlockSpec((TOP_K, tr), lambda i, o: (0, i)),
        ),
        out_shape=jax.ShapeDtypeStruct((TOP_K, t), I32),
        compiler_params=_params(("arbitrary",)),
        name="dispatch_rows",
    )(offs, eidx_t, rank_t)


def _expert_kernel(be_ref, used_ref, src0_ref, srcn_ref, dstp_ref, dstc_ref, h_ref, wg_ref, wu_ref, wd_ref,
                   out_ref, wg_s, wu_s, wd_s, xbuf, ybuf, gsem, ssem, *, spare_row):
    n = pl.program_id(0)
    used = used_ref[0]
    slot = n % 2
    other = 1 - slot

    def gather(src_ref, s):
        for r in range(EXPERT_BLOCK):
            pltpu.make_async_copy(h_ref.at[src_ref[0, 0, r]], xbuf.at[s, pl.ds(r, 1)],
                                  gsem.at[s]).start(priority=r % 2)

    def scatter(dst_ref, s):
        for r in range(EXPERT_BLOCK):
            pltpu.make_async_copy(ybuf.at[s, pl.ds(r, 1)], out_ref.at[pl.ds(dst_ref[0, 0, r], 1)],
                                  ssem.at[s]).start(priority=r % 2)

    def gather_wait(s):
        pltpu.make_async_copy(out_ref.at[pl.ds(0, EXPERT_BLOCK)], xbuf.at[s], gsem.at[s]).wait()

    def scatter_wait(s):
        pltpu.make_async_copy(ybuf.at[s], out_ref.at[pl.ds(0, EXPERT_BLOCK)], ssem.at[s]).wait()

    @pl.when(n == 0)
    def _():
        gather(src0_ref, 0)
        ybuf[...] = jnp.zeros_like(ybuf)
        pltpu.make_async_copy(ybuf.at[0], out_ref.at[pl.ds(spare_row, EXPERT_BLOCK)], ssem.at[0]).start()

    @pl.when((n < used) & ((n == 0) | (be_ref[n] != be_ref[jnp.maximum(n - 1, 0)])))
    def _():
        wg_s[...] = wg_ref[0].astype(BF16)
        wu_s[...] = wu_ref[0].astype(BF16)
        wd_s[...] = wd_ref[0].astype(BF16)

    def step(cur, oth):
        gather_wait(cur)
        scatter_wait(cur)
        gather(srcn_ref, oth)
        scatter(dstp_ref, oth)
        lo, hi = _unpack_bf16_pair(xbuf[cur])
        lo = lo.astype(BF16)
        hi = hi.astype(BF16)
        half = lo.shape[-1]
        g = (jnp.dot(lo, wg_s[:half, :], preferred_element_type=F32)
             + jnp.dot(hi, wg_s[half:, :], preferred_element_type=F32))
        u = (jnp.dot(lo, wu_s[:half, :], preferred_element_type=F32)
             + jnp.dot(hi, wu_s[half:, :], preferred_element_type=F32))
        hid = (_silu(g) * u).astype(BF16)
        y = jnp.dot(hid, wd_s[...], preferred_element_type=F32)
        ybuf[cur] = _pack_bf16_pair(y[:, :half], y[:, half:])

    @pl.when((n < used) & (slot == 0))
    def _():
        step(0, 1)

    @pl.when((n < used) & (slot == 1))
    def _():
        step(1, 0)

    @pl.when(n == used - 1)
    def _():
        scatter(dstc_ref, slot)
        scatter_wait(other)
        scatter_wait(slot)
        gather_wait(other)


def _experts(block_expert, used, row_src, row_dst, h2p, wg, wu, wd, *, layer, n_out_rows, spare_row):
    n_blocks = row_src.shape[0]
    half = h2p.shape[-1]
    _, _, d, hid = wg.shape
    last = lambda n, used: jnp.minimum(n, used[0] - 1)
    wsel = lambda n, be, used: (layer, be[last(n, used)], 0, 0)
    smem = lambda imap: pl.BlockSpec((1, 1, EXPERT_BLOCK), imap, memory_space=pltpu.SMEM)
    return pl.pallas_call(
        functools.partial(_expert_kernel, spare_row=spare_row),
        grid_spec=pltpu.PrefetchScalarGridSpec(
            num_scalar_prefetch=2,
            grid=(n_blocks,),
            in_specs=[
                smem(lambda n, be, used: (0, 0, 0)),
                smem(lambda n, be, used: (jnp.minimum(n + 1, used[0] - 1), 0, 0)),
                smem(lambda n, be, used: (last(n, used), 0, 0)),
                smem(lambda n, be, used: (last(n, used) + 1, 0, 0)),
                pl.BlockSpec(memory_space=pl.ANY),
                pl.BlockSpec((None, 1, d, hid), wsel),
                pl.BlockSpec((None, 1, d, hid), wsel),
                pl.BlockSpec((None, 1, hid, d), wsel),
            ],
            out_specs=pl.BlockSpec(memory_space=pl.ANY),
            scratch_shapes=[pltpu.VMEM((d, hid), BF16), pltpu.VMEM((d, hid), BF16),
                            pltpu.VMEM((hid, d), BF16),
                            pltpu.VMEM((2, EXPERT_BLOCK, half), U32), pltpu.VMEM((2, EXPERT_BLOCK, half), U32),
                            pltpu.SemaphoreType.DMA((2,)), pltpu.SemaphoreType.DMA((2,))],
        ),
        out_shape=jax.ShapeDtypeStruct((n_out_rows, half), U32),
        compiler_params=_params(("arbitrary",)),
        name="expert_ffn",
    )(block_expert, used, row_src, row_src, row_dst, row_dst, h2p, wg, wu, wd)


def _combine_kernel(x_ref, hp_ref, gate_ref, *rest, td, final_norm):
    y_refs = rest[:TOP_K]
    wsg_ref, wsu_ref, wsd_ref, g2_ref, fng_ref, o_ref = rest[TOP_K:]
    lo, hi = _unpack_bf16_pair(hp_ref[...])
    lo = lo.astype(BF16)
    hi = hi.astype(BF16)
    half = lo.shape[-1]
    g = (jnp.dot(lo, wsg_ref[:half, :], preferred_element_type=F32)
         + jnp.dot(hi, wsg_ref[half:, :], preferred_element_type=F32))
    u = (jnp.dot(lo, wsu_ref[:half, :], preferred_element_type=F32)
         + jnp.dot(hi, wsu_ref[half:, :], preferred_element_type=F32))
    hid = (_silu(g) * u).astype(BF16)
    shared = jnp.dot(hid, wsd_ref[...], preferred_element_type=F32)

    gate = gate_ref[...]
    r_lo = jnp.zeros((td, half), F32)
    r_hi = jnp.zeros((td, half), F32)
    for k in range(TOP_K):
        y_lo, y_hi = _unpack_bf16_pair(y_refs[k][...])
        w = gate[:, k:k + 1]
        r_lo = r_lo + w * y_lo
        r_hi = r_hi + w * y_hi
    routed = jnp.concatenate([r_lo, r_hi], axis=-1)
    out = x_ref[...] + g2_ref[0] * (routed + shared)
    if final_norm:
        out = _rms(out, fng_ref[...])
    o_ref[...] = out


def _combine(x2, h2p, gate_rows, y8, wsg, wsu, wsd, mod, fng, *, seq, td, final_norm):
    t, d = x2.shape
    half = d // 2
    nb = seq // td
    n_tb = t // td
    full = lambda a: pl.BlockSpec(a.shape, lambda i: (0,) * a.ndim)
    y_specs = [pl.BlockSpec((td, half), functools.partial(lambda i, k: (k * n_tb + i, 0), k=k))
               for k in range(TOP_K)]
    return pl.pallas_call(
        functools.partial(_combine_kernel, td=td, final_norm=final_norm),
        grid=(n_tb,),
        in_specs=[
            pl.BlockSpec((td, d), lambda i: (i, 0)),
            pl.BlockSpec((td, half), lambda i: (i, 0)),
            pl.BlockSpec((td, TOP_K), lambda i: (i, 0)),
            *y_specs,
            full(wsg), full(wsu), full(wsd),
            pl.BlockSpec((1, 1, d), lambda i: (i // nb, 0, 5)),
            full(fng),
        ],
        out_specs=pl.BlockSpec((td, d), lambda i: (i, 0)),
        out_shape=jax.ShapeDtypeStruct((t, d), F32),
        compiler_params=_params(("arbitrary",)),
        name="shared_combine",
    )(x2, h2p, gate_rows, *([y8] * TOP_K), wsg, wsu, wsd, mod, fng)


def kernel(x, c, w_ada, b_ada, norm1_g, norm2_g, w_in, b_forget, sc_conv_w, cf_conv_w, cf_conv_b,
           cf_ln_g, cf_ln_b, out_norm_g, w_out, w_router, router_bias, w_gate, w_up, w_down,
           ws_gate, ws_up, ws_down, final_norm_g):
    batch, seq, d = x.shape
    depth = w_ada.shape[0]
    t = batch * seq
    n_heads = b_forget.shape[1]
    attn_w = n_heads * HEAD_DIM
    sc_w = sc_conv_w.shape[2]
    cf_w = cf_conv_w.shape[2]
    conv_w = 3 * sc_w + 2 * cf_w
    assert w_in.shape[2] == 3 * attn_w + n_heads + conv_w
    assert batch <= V7X_SUBLANES and n_heads <= V7X_SUBLANES
    tl = _tiles(seq)
    n_assign = t * TOP_K
    n_blocks = -(-(n_assign + N_EXPERTS * (EXPERT_BLOCK - 1)) // EXPERT_BLOCK)
    n_rows = n_blocks * EXPERT_BLOCK

    c_pad = jnp.pad(c, ((0, V7X_SUBLANES - batch), (0, 0)))
    mod_all = _ada(c_pad, w_ada, b_ada)

    x2 = x.reshape(t, d)
    for l in range(depth):
        mod = mod_all[l, :batch].reshape(batch, 1, N_MOD * d)
        wl = w_in[l]
        w_t = jnp.concatenate([wl[:, :attn_w], wl[:, 2 * attn_w:3 * attn_w]], axis=1).T.astype(BF16)
        w_n = jnp.concatenate([wl[:, attn_w:2 * attn_w], wl[:, 3 * attn_w + n_heads:]], axis=1).astype(BF16)
        w_fcols = wl[:, 3 * attn_w:3 * attn_w + n_heads]
        w_f = jnp.pad(w_fcols, ((0, 0), (0, V7X_LANES - n_heads))).astype(BF16)
        w_ft = jnp.pad(w_fcols.T, ((0, 2 * V7X_SUBLANES - n_heads), (0, 0))).astype(BF16)
        qvt, k, conv_in, f_rows, f_t = _inproj(
            x2, norm1_g[l].reshape(1, d), mod, w_t, w_n, w_f, w_ft,
            seq=seq, attn_width=attn_w, conv_width=conv_w, tm=tl["tm_in"], tn=tl["tn_in"])

        bias_row = jnp.pad(b_forget[l], (0, V7X_LANES - n_heads)).reshape(1, V7X_LANES)
        bias_col = jnp.broadcast_to(
            jnp.pad(b_forget[l], (0, 2 * V7X_SUBLANES - n_heads))[:, None], (2 * V7X_SUBLANES, V7X_LANES))
        kf, qf = _fcum(f_rows, f_t, bias_row, bias_col, seq=seq, tc=tl["tc"], n_heads=n_heads)

        y_attn = _attention(qvt, qf, k, kf, batch=batch, seq=seq, n_heads=n_heads, tq=tl["tq"])

        scw = jnp.pad(sc_conv_w[l], ((0, V7X_SUBLANES - SC_KERNEL), (0, 0)))
        cfw = jnp.pad(cf_conv_w[l], ((0, CF_HALO - CF_KERNEL), (0, 0)))
        wr_t = w_router[l].T
        wr_hi = wr_t.astype(BF16)
        wr_lo = (wr_t - wr_hi.astype(F32)).astype(BF16)
        x2, h2p, logits_t = _mixer_out(
            x2, y_attn, conv_in, scw, cfw, cf_conv_b[l].reshape(1, cf_w), cf_ln_g[l].reshape(1, cf_w),
            cf_ln_b[l].reshape(1, cf_w), out_norm_g[l].reshape(1, d), w_out[l].astype(BF16), mod,
            norm2_g[l].reshape(1, d), wr_hi, wr_lo, seq=seq, tm=tl["tm_mix"])

        bias_b = jnp.broadcast_to(router_bias[l][:, None], (N_EXPERTS, tl["tr"]))
        eidx_t, gate_t, rank_t, counts = _router(logits_t, bias_b, tr=tl["tr"])
        counts = counts[:, 0]
        padded = (counts + EXPERT_BLOCK - 1) // EXPERT_BLOCK * EXPERT_BLOCK
        pad_end = jnp.cumsum(padded).astype(I32)
        offs = pad_end - padded
        dest_t = _dest(offs, eidx_t, rank_t, tr=tl["tr"])
        block_start = jnp.arange(n_blocks, dtype=I32) * EXPERT_BLOCK
        block_expert = jnp.minimum(
            jnp.sum((pad_end[None, :] <= block_start[:, None]).astype(I32), axis=1), N_EXPERTS - 1)
        used = (pad_end[-1:] // EXPERT_BLOCK).astype(I32)

        compact = jnp.argsort(dest_t.reshape(-1)).astype(I32)
        compact = jnp.concatenate([compact, jnp.zeros((EXPERT_BLOCK,), I32)])
        first = jnp.cumsum(counts).astype(I32) - counts
        in_expert = block_start - offs[block_expert]
        win_start = jnp.clip(first[block_expert] + in_expert, 0, n_assign)
        n_real = jnp.clip(counts[block_expert] - in_expert, 0, EXPERT_BLOCK)
        window = jax.vmap(lambda s: lax.dynamic_slice(compact, (s,), (EXPERT_BLOCK,)))(win_start)
        lane = jnp.arange(EXPERT_BLOCK, dtype=I32)[None, :]
        is_real = lane < n_real[:, None]
        row_src = jnp.where(is_real, window % t, 0).reshape(n_blocks, 1, EXPERT_BLOCK)
        row_dst = jnp.where(is_real, window, n_assign + block_start[:, None] + lane)
        spare_block = n_assign + n_rows + lane
        row_dst = jnp.concatenate([spare_block, row_dst]).reshape(n_blocks + 1, 1, EXPERT_BLOCK)

        y8 = _experts(block_expert, used, row_src, row_dst, h2p.reshape(t, 1, d // 2), w_gate, w_up, w_down,
                      layer=l,
                      n_out_rows=n_assign + n_rows + 2 * EXPERT_BLOCK,
                      spare_row=n_assign + n_rows + EXPERT_BLOCK)
        x2 = _combine(x2, h2p, gate_t.T, y8, ws_gate[l].astype(BF16), ws_up[l].astype(BF16),
                      ws_down[l].astype(BF16), mod, final_norm_g.reshape(1, d),
                      seq=seq, td=tl["td"], final_norm=(l == depth - 1))
    return x2.reshape(batch, seq, d)
```
